```python
import jax, jax.numpy as jnp
from jax import lax
import numpy as np

D_MODEL = 2048
BATCH = 4
SEQ = 4096
DEPTH = 4

CTX_LEN = 256
GRID_W = 64
HEAD_DIM = 128
ATTN_HEADS = 8
KV_HEADS = 2
GQ = ATTN_HEADS // KV_HEADS
ATTN_W = ATTN_HEADS * HEAD_DIM
KV_W = KV_HEADS * HEAD_DIM
WINDOW = 128
BLOCK = 128
ROPE_BASE = 10000.0
CONV_CH = D_MODEL // 2
CONV_K = 31
MIX_W = ATTN_W + CONV_CH
IN_W = ATTN_W + 2 * KV_W + 2 * CONV_CH
D_FF = 4 * D_MODEL
EPS = 1e-6

kernel_name = 'hybrid_window_gqa_conformer_conv_dit_block'


def _rmsnorm(x, g):
    x32 = x.astype(jnp.float32)
    y = x32 * lax.rsqrt(jnp.mean(x32 * x32, axis=-1, keepdims=True) + EPS)
    return (y * g.astype(jnp.float32)).astype(x.dtype)


def _layer_norm(x, g, b):
    x32 = x.astype(jnp.float32)
    mu = jnp.mean(x32, axis=-1, keepdims=True)
    var = jnp.mean(jnp.square(x32 - mu), axis=-1, keepdims=True)
    y = (x32 - mu) * lax.rsqrt(var + EPS)
    return (y * g.astype(jnp.float32) + b.astype(jnp.float32)).astype(x.dtype)


def _modulate(h, shift, scale):
    return h * (1.0 + scale) + shift


def _rotate_half(x):
    x1, x2 = jnp.split(x, 2, axis=-1)
    return jnp.concatenate([-x2, x1], axis=-1)


def _rope(x, cos, sin):
    return x * cos[None, :, None, :] + _rotate_half(x) * sin[None, :, None, :]


def _axial_rope_tables(n_tokens, dtype):
    rows = n_tokens // GRID_W
    t = jnp.arange(rows * GRID_W, dtype=jnp.int32)
    row = (t // GRID_W).astype(jnp.float32)
    col = (t % GRID_W).astype(jnp.float32)
    n_freq = HEAD_DIM // 4
    inv = ROPE_BASE ** (-jnp.arange(n_freq, dtype=jnp.float32) / n_freq)
    theta = jnp.concatenate([row[:, None] * inv, col[:, None] * inv], axis=-1)
    theta = jnp.concatenate([theta, theta], axis=-1)
    return jnp.cos(theta).astype(dtype), jnp.sin(theta).astype(dtype)


def _sink_softmax(scores, sink):
    col = jnp.broadcast_to(sink.astype(jnp.float32).reshape(KV_HEADS, GQ)[:, :, None, None],
                           scores.shape[:-1] + (1,))
    p = jax.nn.softmax(jnp.concatenate([scores, col], axis=-1), axis=-1)
    return p[..., :-1]


def _window_attention(q, k, v, k_ctx, v_ctx, sink):
    B, S = q.shape[0], q.shape[1]
    nb = S // BLOCK
    scale = HEAD_DIM ** -0.5
    qb = q.reshape(B, nb, BLOCK, KV_HEADS, GQ, HEAD_DIM)
    pad = ((0, 0), (BLOCK, BLOCK), (0, 0), (0, 0))
    kp = jnp.pad(k, pad).reshape(B, nb + 2, BLOCK, KV_HEADS, HEAD_DIM)
    vp = jnp.pad(v, pad).reshape(B, nb + 2, BLOCK, KV_HEADS, HEAD_DIM)
    kw = jnp.concatenate([kp[:, :-2], kp[:, 1:-1], kp[:, 2:]], axis=2)
    vw = jnp.concatenate([vp[:, :-2], vp[:, 1:-1], vp[:, 2:]], axis=2)
    s_win = jnp.einsum('bnqkgd,bnjkd->bnkgqj', qb, kw).astype(jnp.float32) * scale
    s_ctx = jnp.einsum('bnqkgd,bckd->bnkgqc', qb, k_ctx).astype(jnp.float32) * scale
    n = jnp.arange(nb)[:, None, None]
    r = jnp.arange(BLOCK)[None, :, None]
    j = jnp.arange(3 * BLOCK)[None, None, :]
    qpos = n * BLOCK + r
    kpos = (n - 1) * BLOCK + j
    mask = (jnp.abs(kpos - qpos) <= WINDOW) & (kpos >= 0) & (kpos < S)
    s_win = jnp.where(mask[None, :, None, None], s_win, -1e30)
    p = _sink_softmax(jnp.concatenate([s_win, s_ctx], axis=-1), sink)
    p_win = p[..., :3 * BLOCK].astype(v.dtype)
    p_ctx = p[..., 3 * BLOCK:].astype(v.dtype)
    out = (jnp.einsum('bnkgqj,bnjkd->bnqkgd', p_win, vw)
           + jnp.einsum('bnkgqc,bckd->bnqkgd', p_ctx, v_ctx))
    return out.reshape(B, S, ATTN_W)


def _context_attention(q, k, v, sink):
    B, C = q.shape[0], q.shape[1]
    qg = q.reshape(B, C, KV_HEADS, GQ, HEAD_DIM)
    s = jnp.einsum('bqkgd,bjkd->bkgqj', qg, k).astype(jnp.float32) * (HEAD_DIM ** -0.5)
    p = _sink_softmax(s, sink).astype(v.dtype)
    return jnp.einsum('bkgqj,bjkd->bqkgd', p, v).reshape(B, C, ATTN_W)


def _conv_module(u, w, b, ln_g, ln_b):
    a, gt = jnp.split(u, 2, axis=-1)
    h = a * jax.nn.sigmoid(gt)
    h = lax.conv_general_dilated(h, w[:, None, :], window_strides=(1,),
                                 padding=[(CONV_K // 2, CONV_K // 2)],
                                 dimension_numbers=('NWC', 'WIO', 'NWC'),
                                 feature_group_count=CONV_CH) + b
    return jax.nn.silu(_layer_norm(h, ln_g, ln_b))


def _mlp(h, w1, w2):
    return jnp.square(jax.nn.relu(h @ w1)) @ w2


def setup_inputs(seed: int = 0) -> dict:
    key = jax.random.key(seed)
    ks = jax.random.split(key, 20)
    f32 = jnp.float32
    nrm = lambda k, shape, s: jax.random.normal(k, shape, f32) * s
    return {
        'x': nrm(ks[0], (BATCH, SEQ, D_MODEL), 1.0),
        'c': nrm(ks[1], (BATCH, D_MODEL), 1.0),
        'ctx': nrm(ks[2], (BATCH, CTX_LEN, D_MODEL), 1.0),
        'c_ctx': nrm(ks[3], (D_MODEL,), 1.0),
        'w_ada': nrm(ks[4], (DEPTH, D_MODEL, 6 * D_MODEL), 0.5 * D_MODEL ** -0.5),
        'b_ada': nrm(ks[5], (DEPTH, 6 * D_MODEL), 0.02),
        'g_mix': 1.0 + nrm(ks[6], (DEPTH, D_MODEL), 0.1),
        'g_mlp': 1.0 + nrm(ks[7], (DEPTH, D_MODEL), 0.1),
        'w_in': nrm(ks[8], (DEPTH, D_MODEL, IN_W), D_MODEL ** -0.5),
        'attn_sink': nrm(ks[9], (DEPTH, ATTN_HEADS), 1.0),
        'conv_w': nrm(ks[10], (DEPTH, CONV_K, CONV_CH), CONV_K ** -0.5),
        'conv_b': nrm(ks[11], (DEPTH, CONV_CH), 0.02),
        'conv_ln_g': 1.0 + nrm(ks[12], (DEPTH, CONV_CH), 0.1),
        'conv_ln_b': nrm(ks[13], (DEPTH, CONV_CH), 0.02),
        'w_out': nrm(ks[14], (DEPTH, MIX_W, D_MODEL), MIX_W ** -0.5),
        'w_mlp1': nrm(ks[15], (DEPTH, D_MODEL, D_FF), D_MODEL ** -0.5),
        'w_mlp2': nrm(ks[16], (DEPTH, D_FF, D_MODEL), D_FF ** -0.5),
        'g_final': 1.0 + nrm(ks[17], (D_MODEL,), 0.1),
    }


def reference(x, c, ctx, c_ctx, w_ada, b_ada, g_mix, g_mlp, w_in, attn_sink, conv_w, conv_b,
              conv_ln_g, conv_ln_b, w_out, w_mlp1, w_mlp2, g_final):
    B, S = x.shape[0], x.shape[1]
    C = ctx.shape[1]
    cos, sin = _axial_rope_tables(S, x.dtype)
    silu_c = jax.nn.silu(c)
    silu_cc = jax.nn.silu(c_ctx)
    split_pts = [ATTN_W, ATTN_W + KV_W, ATTN_W + 2 * KV_W]
    for l in range(DEPTH):
        last = l == DEPTH - 1
        sh1, sc1, g1, sh2, sc2, g2 = jnp.split((silu_c @ w_ada[l] + b_ada[l])[:, None, :], 6, axis=-1)
        csh1, csc1, cg1, csh2, csc2, cg2 = jnp.split(silu_cc @ w_ada[l] + b_ada[l], 6, axis=-1)

        h = _modulate(_rmsnorm(x, g_mix[l]), sh1, sc1)
        hc = _modulate(_rmsnorm(ctx, g_mix[l]), csh1, csc1)
        q, k, v, u = jnp.split(h @ w_in[l], split_pts, axis=-1)
        q = _rope(q.reshape(B, S, ATTN_HEADS, HEAD_DIM), cos, sin)
        k = _rope(k.reshape(B, S, KV_HEADS, HEAD_DIM), cos, sin)
        v = v.reshape(B, S, KV_HEADS, HEAD_DIM)
        if last:
            kc, vc = jnp.split(hc @ w_in[l][:, ATTN_W:ATTN_W + 2 * KV_W], 2, axis=-1)
        else:
            qc, kc, vc, uc = jnp.split(hc @ w_in[l], split_pts, axis=-1)
        kc = kc.reshape(B, C, KV_HEADS, HEAD_DIM)
        vc = vc.reshape(B, C, KV_HEADS, HEAD_DIM)

        att = _window_attention(q, k, v, kc, vc, attn_sink[l])
        cv = _conv_module(u, conv_w[l], conv_b[l], conv_ln_g[l], conv_ln_b[l])
        x = x + g1 * (jnp.concatenate([att, cv], axis=-1) @ w_out[l])
        if not last:
            att_c = _context_attention(qc.reshape(B, C, ATTN_HEADS, HEAD_DIM), kc, vc, attn_sink[l])
            cv_c = _conv_module(uc, conv_w[l], conv_b[l], conv_ln_g[l], conv_ln_b[l])
            ctx = ctx + cg1 * (jnp.concatenate([att_c, cv_c], axis=-1) @ w_out[l])

        x = x + g2 * _mlp(_modulate(_rmsnorm(x, g_mlp[l]), sh2, sc2), w_mlp1[l], w_mlp2[l])
        if not last:
            ctx = ctx + cg2 * _mlp(_modulate(_rmsnorm(ctx, g_mlp[l]), csh2, csc2), w_mlp1[l], w_mlp2[l])
    return _rmsnorm(x, g_final)
```

```python
import functools

import jax
import jax.numpy as jnp
from jax import lax
from jax.experimental import pallas as pl
from jax.experimental.pallas import tpu as pltpu

F32 = jnp.float32
BF16 = jnp.bfloat16

D_MODEL = 2048
DEPTH = 4
GRID_W = 64
HEAD_DIM = 128
ATTN_HEADS = 8
KV_HEADS = 2
GQ = ATTN_HEADS // KV_HEADS
ATTN_W = ATTN_HEADS * HEAD_DIM
KV_W = KV_HEADS * HEAD_DIM
BLOCK = 128
ROPE_BASE = 10000.0
CONV_CH = D_MODEL // 2
CONV_K = 31
CONV_HALO = 16
MIX_W = ATTN_W + CONV_CH
IN_W = ATTN_W + 2 * KV_W + 2 * CONV_CH
D_FF = 4 * D_MODEL
EPS = 1e-6
NEG = -1e30

V7X_VMEM_LIMIT = 56 * 1024 * 1024
MOD_ROWS = 8

TM_IN = 512
TQ = 512
TC_LAT = 512
TM_OUT = 512
TM_MLP = 1024
FC_MLP = 512
TN_ADA = 1024


def _params(n_axes, vmem=V7X_VMEM_LIMIT):
    return pltpu.CompilerParams(dimension_semantics=("arbitrary",) * n_axes,
                                vmem_limit_bytes=vmem)


def _resident(shape):
    return pl.BlockSpec(shape, lambda *_: (0,) * len(shape), pipeline_mode=pl.Buffered(1))


def _rms_modulate(x, g, shift, scale):
    ms = jnp.mean(x * x, axis=-1, keepdims=True)
    y = x * lax.rsqrt(ms + EPS) * g
    return y * (1.0 + scale) + shift


def _ada_kernel(c_ref, w_ref, b_ref, o_ref):
    c = c_ref[...]
    s = (c * jax.nn.sigmoid(c)).astype(BF16)
    o_ref[...] = jnp.dot(s, w_ref[...].astype(BF16), preferred_element_type=F32) + b_ref[...]


def _ada_table(cc, w_ada, b_ada):
    n_out = 6 * D_MODEL
    return pl.pallas_call(
        _ada_kernel,
        grid=(DEPTH, n_out // TN_ADA),
        in_specs=[
            pl.BlockSpec((MOD_ROWS, D_MODEL), lambda l, j: (0, 0)),
            pl.BlockSpec((None, D_MODEL, TN_ADA), lambda l, j: (l, 0, j)),
            pl.BlockSpec((None, 1, TN_ADA), lambda l, j: (l, 0, j)),
        ],
        out_specs=pl.BlockSpec((None, MOD_ROWS, TN_ADA), lambda l, j: (l, 0, j)),
        out_shape=jax.ShapeDtypeStruct((DEPTH, MOD_ROWS, n_out), F32),
        compiler_params=_params(2),
        name="ada_table",
    )(cc, w_ada, b_ada.reshape(DEPTH, 1, n_out))


def _inproj_kernel(x_ref, mod_ref, g_ref, cos_ref, sin_ref, w_ref,
                   q_ref, k_ref, v_ref, u_ref, h_s):
    h = _rms_modulate(x_ref[...], g_ref[...],
                      mod_ref[:, 0:D_MODEL], mod_ref[:, D_MODEL:2 * D_MODEL])
    h_s[...] = h.astype(BF16)
    cos = cos_ref[...]
    sin = sin_ref[...]

    def rope(seg):
        return seg * cos + pltpu.roll(seg, HEAD_DIM // 2, 1) * sin

    nq = 512
    for c in range(ATTN_W // nq):
        r = jnp.dot(h_s[...], w_ref[:, c * nq:(c + 1) * nq], preferred_element_type=F32)
        for hh in range(nq // HEAD_DIM):
            lo = hh * HEAD_DIM
            q_ref[:, c * nq + lo:c * nq + lo + HEAD_DIM] = rope(r[:, lo:lo + HEAD_DIM]).astype(BF16)
    r = jnp.dot(h_s[...], w_ref[:, ATTN_W:ATTN_W + 2 * KV_W], preferred_element_type=F32)
    for hh in range(KV_HEADS):
        lo = hh * HEAD_DIM
        k_ref[:, lo:lo + HEAD_DIM] = rope(r[:, lo:lo + HEAD_DIM]).astype(BF16)
    v_ref[...] = r[:, KV_W:2 * KV_W].astype(BF16)
    nu = 256
    u0 = ATTN_W + 2 * KV_W
    for c in range(CONV_CH // nu):
        a = jnp.dot(h_s[...], w_ref[:, u0 + c * nu:u0 + (c + 1) * nu], preferred_element_type=F32)
        gt = jnp.dot(h_s[...], w_ref[:, u0 + CONV_CH + c * nu:u0 + CONV_CH + (c + 1) * nu],
                     preferred_element_type=F32)
        u_ref[:, c * nu:(c + 1) * nu] = a * jax.nn.sigmoid(gt)


def _inproj(xa, mod_l, g, cos_t, sin_t, w, n_lat_rows, seq):
    rows = xa.shape[0]
    tm = TM_IN
    n_lat = n_lat_rows // tm
    per_seq = seq // tm
    n_batch = n_lat_rows // seq

    def mod_idx(i):
        return (jnp.minimum(i * tm // seq, n_batch), 0, 0)

    def rope_idx(i):
        return (jnp.where(i < n_lat, i % per_seq, per_seq), 0)

    row = lambda i: (i, 0)
    return pl.pallas_call(
        _inproj_kernel,
        grid=(rows // tm,),
        in_specs=[
            pl.BlockSpec((tm, D_MODEL), row),
            pl.BlockSpec((None, 1, 6 * D_MODEL), mod_idx),
            _resident((1, D_MODEL)),
            pl.BlockSpec((tm, HEAD_DIM), rope_idx),
            pl.BlockSpec((tm, HEAD_DIM), rope_idx),
            _resident((D_MODEL, IN_W)),
        ],
        out_specs=[
            pl.BlockSpec((tm, ATTN_W), row),
            pl.BlockSpec((tm, KV_W), row),
            pl.BlockSpec((tm, KV_W), row),
            pl.BlockSpec((tm, CONV_CH), row),
        ],
        out_shape=[
            jax.ShapeDtypeStruct((rows, ATTN_W), BF16),
            jax.ShapeDtypeStruct((rows, KV_W), BF16),
            jax.ShapeDtypeStruct((rows, KV_W), BF16),
            jax.ShapeDtypeStruct((rows, CONV_CH), F32),
        ],
        scratch_shapes=[pltpu.VMEM((tm, D_MODEL), BF16)],
        compiler_params=_params(1),
        name="inproj",
    )(xa, mod_l, g, cos_t, sin_t, w)


def _nt_dot(a, b):
    return lax.dot_general(a, b, (((1,), (1,)), ((), ())), preferred_element_type=F32)


def _stack_heads(q_ref, r0, nrows, kvh):
    return jnp.concatenate(
        [q_ref[r0:r0 + nrows, (kvh * GQ + g) * HEAD_DIM:(kvh * GQ + g + 1) * HEAD_DIM] for g in range(GQ)],
        axis=0)


def _sink_column(sink_ref, nrows, kvh):
    return jnp.concatenate(
        [jnp.full((nrows, 1), sink_ref[kvh * GQ + g], F32) for g in range(GQ)], axis=0)


def _attn_kernel(sink_ref, q_ref, km_ref, kp_ref, kn_ref, vm_ref, vp_ref, vn_ref, kc_ref, vc_ref,
                 o_ref, kbuf, vbuf, *, n_lat_steps, steps_per_seq, ctx_len):
    i = pl.program_id(0)
    scale = HEAD_DIM ** -0.5
    win = 3 * BLOCK

    def store_heads(o, r0, nrows, kvh):
        for g in range(GQ):
            c0 = (kvh * GQ + g) * HEAD_DIM
            o_ref[r0:r0 + nrows, c0:c0 + HEAD_DIM] = o[g * nrows:(g + 1) * nrows].astype(BF16)

    @pl.when(i < n_lat_steps)
    def _latent():
        kbuf[0:BLOCK] = kp_ref[...]
        kbuf[BLOCK:BLOCK + TQ] = km_ref[...]
        kbuf[BLOCK + TQ:2 * BLOCK + TQ] = kn_ref[...]
        vbuf[0:BLOCK] = vp_ref[...]
        vbuf[BLOCK:BLOCK + TQ] = vm_ref[...]
        vbuf[BLOCK + TQ:2 * BLOCK + TQ] = vn_ref[...]
        pos = i % steps_per_seq
        first_col = jnp.where(pos == 0, BLOCK, 0)
        end_col = jnp.where(pos == steps_per_seq - 1, 2 * BLOCK, win)
        r = lax.broadcasted_iota(jnp.int32, (GQ * BLOCK, win), 0) & (BLOCK - 1)
        j = lax.broadcasted_iota(jnp.int32, (GQ * BLOCK, win), 1)
        band = (j >= r) & (j <= r + 2 * BLOCK)
        nblk = TQ // BLOCK
        for blk in range(nblk):
            mask = band
            if blk == 0:
                mask = band & (j >= first_col)
            if blk == nblk - 1:
                mask = band & (j < end_col)
            r0 = blk * BLOCK
            for kvh in range(KV_HEADS):
                c0 = kvh * HEAD_DIM
                qg = _stack_heads(q_ref, r0, BLOCK, kvh)
                kw = kbuf[r0:r0 + win, c0:c0 + HEAD_DIM]
                vw = vbuf[r0:r0 + win, c0:c0 + HEAD_DIM]
                kc = kc_ref[:, c0:c0 + HEAD_DIM]
                vc = vc_ref[:, c0:c0 + HEAD_DIM]
                s_w = jnp.where(mask, _nt_dot(qg, kw) * scale, NEG)
                s_c = _nt_dot(qg, kc) * scale
                sink = _sink_column(sink_ref, BLOCK, kvh)
                m = jnp.maximum(jnp.maximum(jnp.max(s_w, axis=-1, keepdims=True),
                                            jnp.max(s_c, axis=-1, keepdims=True)), sink)
                e_w = jnp.exp(s_w - m)
                e_c = jnp.exp(s_c - m)
                den = (jnp.sum(e_w, axis=-1, keepdims=True) + jnp.sum(e_c, axis=-1, keepdims=True)
                       + jnp.exp(sink - m))
                o = (jnp.dot(e_w.astype(BF16), vw, preferred_element_type=F32)
                     + jnp.dot(e_c.astype(BF16), vc, preferred_element_type=F32))
                store_heads(o / den, r0, BLOCK, kvh)

    @pl.when(i >= n_lat_steps)
    def _context():
        for bb in range(TQ // ctx_len):
            r0 = bb * ctx_len
            for kvh in range(KV_HEADS):
                c0 = kvh * HEAD_DIM
                qg = _stack_heads(q_ref, r0, ctx_len, kvh)
                kc = km_ref[r0:r0 + ctx_len, c0:c0 + HEAD_DIM]
                vc = vm_ref[r0:r0 + ctx_len, c0:c0 + HEAD_DIM]
                s = _nt_dot(qg, kc) * scale
                sink = _sink_column(sink_ref, ctx_len, kvh)
                m = jnp.maximum(jnp.max(s, axis=-1, keepdims=True), sink)
                e = jnp.exp(s - m)
                den = jnp.sum(e, axis=-1, keepdims=True) + jnp.exp(sink - m)
                o = jnp.dot(e.astype(BF16), vc, preferred_element_type=F32)
                store_heads(o / den, r0, ctx_len, kvh)


def _attention(sink, q, k, v, n_lat_rows, seq, ctx_len, with_ctx):
    rows = q.shape[0]
    n_lat_steps = n_lat_rows // TQ
    steps = rows // TQ if with_ctx else n_lat_steps
    steps_per_seq = seq // TQ
    n_batch = n_lat_rows // seq
    hb = TQ // BLOCK
    last_blk = rows // BLOCK - 1

    main = lambda i: (i, 0)
    prev = lambda i: (jnp.maximum(i * hb - 1, 0), 0)
    nxt = lambda i: (jnp.minimum((i + 1) * hb, last_blk), 0)
    ctx = lambda i: (n_lat_rows // ctx_len + jnp.minimum(i // steps_per_seq, n_batch - 1), 0)
    kv_specs = [pl.BlockSpec((TQ, KV_W), main), pl.BlockSpec((BLOCK, KV_W), prev),
                pl.BlockSpec((BLOCK, KV_W), nxt)]
    return pl.pallas_call(
        functools.partial(_attn_kernel, n_lat_steps=n_lat_steps, steps_per_seq=steps_per_seq,
                          ctx_len=ctx_len),
        grid=(steps,),
        in_specs=[pl.BlockSpec(memory_space=pltpu.SMEM), pl.BlockSpec((TQ, ATTN_W), main)]
                 + kv_specs + kv_specs
                 + [pl.BlockSpec((ctx_len, KV_W), ctx), pl.BlockSpec((ctx_len, KV_W), ctx)],
        out_specs=pl.BlockSpec((TQ, ATTN_W), main),
        out_shape=jax.ShapeDtypeStruct((steps * TQ, ATTN_W), BF16),
        scratch_shapes=[pltpu.VMEM((TQ + 2 * BLOCK, KV_W), BF16),
                        pltpu.VMEM((TQ + 2 * BLOCK, KV_W), BF16)],
        compiler_params=_params(1),
        name="attention",
    )(sink, q, k, k, k, v, v, v, k, v)


def _conv_kernel(u_ref, up_ref, un_ref, w_ref, b_ref, lg_ref, lb_ref, o_ref,
                 ext, ysh, cbuf, *, tc, tiles_per_seq):
    i = pl.program_id(0)
    pos = i % tiles_per_seq
    has_prev = pos != 0
    has_next = pos != tiles_per_seq - 1
    ext[0:CONV_HALO] = jnp.where(has_prev, up_ref[...], 0.0)
    ext[CONV_HALO:CONV_HALO + tc] = u_ref[...]
    ext[CONV_HALO + tc:2 * CONV_HALO + tc] = jnp.where(has_next, un_ref[...], 0.0)
    n_sh = tc + 24
    for s in range(8):
        ysh[s] = ext[s:s + n_sh, :]
    rc = min(tc, 256)
    lanes = 128
    first = CONV_HALO - CONV_K // 2
    for cc in range(CONV_CH // lanes):
        c0 = cc * lanes
        for rb in range(tc // rc):
            acc = jnp.zeros((rc, lanes), F32)
            for tap in range(CONV_K):
                off = first + tap
                q8, s = divmod(off, 8)
                acc = acc + ysh[s, 8 * q8 + rb * rc:8 * q8 + rb * rc + rc, c0:c0 + lanes] \
                    * w_ref[tap:tap + 1, c0:c0 + lanes]
            cbuf[rb * rc:(rb + 1) * rc, c0:c0 + lanes] = acc + b_ref[:, c0:c0 + lanes]
    h = cbuf[...]
    mu = jnp.mean(h, axis=-1, keepdims=True)
    d = h - mu
    var = jnp.mean(d * d, axis=-1, keepdims=True)
    y = d * lax.rsqrt(var + EPS) * lg_ref[...] + lb_ref[...]
    o_ref[...] = (y * jax.nn.sigmoid(y)).astype(BF16)


def _conv_call(u, prev_out, w, b, lg, lb, tc, row0, n_rows, tiles_per_seq):
    rows = u.shape[0]
    t0 = row0 // tc
    hb = tc // CONV_HALO
    last_h = rows // CONV_HALO - 1
    main = lambda i: (t0 + i, 0)
    prev = lambda i: (jnp.maximum((t0 + i) * hb - 1, 0), 0)
    nxt = lambda i: (jnp.minimum((t0 + i + 1) * hb, last_h), 0)
    in_specs = [
        pl.BlockSpec((tc, CONV_CH), main),
        pl.BlockSpec((CONV_HALO, CONV_CH), prev),
        pl.BlockSpec((CONV_HALO, CONV_CH), nxt),
        _resident((CONV_K, CONV_CH)),
        _resident((1, CONV_CH)),
        _resident((1, CONV_CH)),
        _resident((1, CONV_CH)),
    ]
    args = [u, u, u, w, b, lg, lb]
    aliases = {}
    if prev_out is not None:
        in_specs.append(pl.BlockSpec(memory_space=pl.ANY))
        args.append(prev_out)
        aliases = {len(args) - 1: 0}
    kern = functools.partial(_conv_kernel, tc=tc, tiles_per_seq=tiles_per_seq)
    if prev_out is not None:
        kern = functools.partial(_drop_alias_ref, kern, len(args) - 1)
    return pl.pallas_call(
        kern,
        grid=(n_rows // tc,),
        in_specs=in_specs,
        out_specs=pl.BlockSpec((tc, CONV_CH), main),
        out_shape=jax.ShapeDtypeStruct((rows, CONV_CH), BF16),
        scratch_shapes=[pltpu.VMEM((tc + 2 * CONV_HALO, CONV_CH), F32),
                        pltpu.VMEM((8, tc + 24, CONV_CH), F32),
                        pltpu.VMEM((tc, CONV_CH), F32)],
        input_output_aliases=aliases,
        compiler_params=_params(1),
        name="conv_ctx" if prev_out is not None else "conv_lat",
    )(*args)


def _drop_alias_ref(kern, pos, *refs):
    return kern(*refs[:pos], *refs[pos + 1:])


def _outproj_kernel(att_ref, cv_ref, x_ref, mod_ref, w_ref, o_ref):
    mix = jnp.concatenate([att_ref[...], cv_ref[...]], axis=1)
    nn = 512
    for c in range(D_MODEL // nn):
        r = jnp.dot(mix, w_ref[:, c * nn:(c + 1) * nn], preferred_element_type=F32)
        g1 = mod_ref[:, 2 * D_MODEL + c * nn:2 * D_MODEL + (c + 1) * nn]
        o_ref[:, c * nn:(c + 1) * nn] = x_ref[:, c * nn:(c + 1) * nn] + g1 * r


def _outproj(att, cv, xa, mod_l, w, n_rows, n_lat_rows, seq):
    tm = TM_OUT
    n_batch = n_lat_rows // seq
    row = lambda i: (i, 0)
    mod_idx = lambda i: (jnp.minimum(i * tm // seq, n_batch), 0, 0)
    alias = {2: 0} if n_rows == xa.shape[0] else {}
    return pl.pallas_call(
        _outproj_kernel,
        grid=(n_rows // tm,),
        in_specs=[
            pl.BlockSpec((tm, ATTN_W), row),
            pl.BlockSpec((tm, CONV_CH), row),
            pl.BlockSpec((tm, D_MODEL), row),
            pl.BlockSpec((None, 1, 6 * D_MODEL), mod_idx),
            _resident((MIX_W, D_MODEL)),
        ],
        out_specs=pl.BlockSpec((tm, D_MODEL), row),
        out_shape=jax.ShapeDtypeStruct((n_rows, D_MODEL), F32),
        input_output_aliases=alias,
        compiler_params=_params(1),
        name="outproj",
    )(att, cv, xa, mod_l, w)


def _mlp_kernel(x_ref, mod_ref, g_ref, gf_ref, w1_ref, w2_ref, o_ref, h_s, *, final):
    j = pl.program_id(1)

    @pl.when(j == 0)
    def _prologue():
        h = _rms_modulate(x_ref[...], g_ref[...],
                          mod_ref[:, 3 * D_MODEL:4 * D_MODEL], mod_ref[:, 4 * D_MODEL:5 * D_MODEL])
        h_s[...] = h.astype(BF16)
        o_ref[...] = jnp.zeros_like(o_ref)

    a = jnp.dot(h_s[...], w1_ref[...], preferred_element_type=F32)
    a = jnp.maximum(a, 0.0)
    o_ref[...] += jnp.dot((a * a).astype(BF16), w2_ref[...], preferred_element_type=F32)

    @pl.when(j == pl.num_programs(1) - 1)
    def _epilogue():
        y = x_ref[...] + mod_ref[:, 5 * D_MODEL:6 * D_MODEL] * o_ref[...]
        if final:
            ms = jnp.mean(y * y, axis=-1, keepdims=True)
            y = y * lax.rsqrt(ms + EPS) * gf_ref[...]
        o_ref[...] = y


def _mlp(xa, mod_l, g, gf, w1, w2, n_lat_rows, seq, final):
    rows = xa.shape[0]
    tm, fc = TM_MLP, FC_MLP
    n_batch = n_lat_rows // seq
    row = lambda i, j: (i, 0)
    mod_idx = lambda i, j: (jnp.minimum(i * tm // seq, n_batch), 0, 0)
    return pl.pallas_call(
        functools.partial(_mlp_kernel, final=final),
        grid=(rows // tm, D_FF // fc),
        in_specs=[
            pl.BlockSpec((tm, D_MODEL), row),
            pl.BlockSpec((None, 1, 6 * D_MODEL), mod_idx),
            pl.BlockSpec((1, D_MODEL), lambda i, j: (0, 0)),
            pl.BlockSpec((1, D_MODEL), lambda i, j: (0, 0)),
            pl.BlockSpec((D_MODEL, fc), lambda i, j: (0, j)),
            pl.BlockSpec((fc, D_MODEL), lambda i, j: (j, 0)),
        ],
        out_specs=pl.BlockSpec((tm, D_MODEL), row),
        out_shape=jax.ShapeDtypeStruct((rows, D_MODEL), F32),
        scratch_shapes=[pltpu.VMEM((tm, D_MODEL), BF16)],
        input_output_aliases={0: 0},
        compiler_params=_params(2),
        name="mlp",
    )(xa, mod_l, g, gf, w1, w2)


def _rope_tables(seq, pad_rows):
    t = jnp.arange(seq, dtype=jnp.int32)
    row = (t // GRID_W).astype(F32)
    col = (t % GRID_W).astype(F32)
    n_freq = HEAD_DIM // 4
    inv = ROPE_BASE ** (-jnp.arange(n_freq, dtype=F32) / n_freq)
    theta = jnp.concatenate([row[:, None] * inv, col[:, None] * inv], axis=-1)
    theta = jnp.concatenate([theta, theta], axis=-1)
    sign = jnp.concatenate([-jnp.ones((HEAD_DIM // 2,), F32), jnp.ones((HEAD_DIM // 2,), F32)])
    cos = jnp.concatenate([jnp.cos(theta), jnp.ones((pad_rows, HEAD_DIM), F32)], axis=0)
    sin = jnp.concatenate([jnp.sin(theta) * sign, jnp.zeros((pad_rows, HEAD_DIM), F32)], axis=0)
    return cos, sin


def kernel(x, c, ctx, c_ctx, w_ada, b_ada, g_mix, g_mlp, w_in, attn_sink, conv_w, conv_b,
           conv_ln_g, conv_ln_b, w_out, w_mlp1, w_mlp2, g_final):
    n_batch, seq, d = x.shape
    ctx_len = ctx.shape[1]
    n_lat_rows = n_batch * seq
    assert d == D_MODEL and n_batch + 1 <= MOD_ROWS
    assert seq % TM_MLP == 0 and (n_batch * ctx_len) % TM_MLP == 0 and TQ % ctx_len == 0

    xa = jnp.concatenate([x.reshape(n_lat_rows, d), ctx.reshape(n_batch * ctx_len, d)], axis=0)
    cc = jnp.concatenate([c, c_ctx[None], jnp.zeros((MOD_ROWS - n_batch - 1, d), F32)], axis=0)
    mod = _ada_table(cc, w_ada, b_ada).reshape(DEPTH, MOD_ROWS, 1, 6 * d)
    cos_t, sin_t = _rope_tables(seq, TM_IN)

    w_in_b = w_in.astype(BF16)
    w_out_b = w_out.astype(BF16)
    w1_b = w_mlp1.astype(BF16)
    w2_b = w_mlp2.astype(BF16)
    gf = g_final.reshape(1, d)

    for l in range(DEPTH):
        last = l == DEPTH - 1
        q, k, v, u = _inproj(xa, mod[l], g_mix[l].reshape(1, d), cos_t, sin_t, w_in_b[l],
                             n_lat_rows, seq)
        att = _attention(attn_sink[l], q, k, v, n_lat_rows, seq, ctx_len, with_ctx=not last)
        cw, cb = conv_w[l], conv_b[l].reshape(1, CONV_CH)
        lg, lb = conv_ln_g[l].reshape(1, CONV_CH), conv_ln_b[l].reshape(1, CONV_CH)
        cv = _conv_call(u, None, cw, cb, lg, lb, TC_LAT, 0, n_lat_rows, seq // TC_LAT)
        if not last:
            cv = _conv_call(u, cv, cw, cb, lg, lb, ctx_len, n_lat_rows, n_batch * ctx_len, 1)
        n_rows = n_lat_rows if last else xa.shape[0]
        xa = _outproj(att, cv, xa, mod[l], w_out_b[l], n_rows, n_lat_rows, seq)
        xa = _mlp(xa, mod[l], g_mlp[l].reshape(1, d), gf, w1_b[l], w2_b[l], n_lat_rows, seq, last)
    return xa.reshape(n_batch, seq, d)
```

```python
import functools
import math

import jax
import jax.numpy as jnp
from jax import lax
from jax.experimental import pallas as pl
from jax.experimental.pallas import tpu as pltpu

F32 = jnp.float32
BF16 = jnp.bfloat16

D_MODEL = 2048
DEPTH = 4
GRID_W = 64
HEAD_DIM = 128
ATTN_HEADS = 8
KV_HEADS = 2
GQ = ATTN_HEADS // KV_HEADS
ATTN_W = ATTN_HEADS * HEAD_DIM
KV_W = KV_HEADS * HEAD_DIM
BLOCK = 128
ROPE_BASE = 10000.0
CONV_CH = D_MODEL // 2
CONV_K = 31
CONV_HALO = 16
MIX_W = ATTN_W + CONV_CH
IN_W = ATTN_W + 2 * KV_W + 2 * CONV_CH
D_FF = 4 * D_MODEL
EPS = 1e-6
NEG = -1e30

SUBLANES = 8
LANES = 128
V7X_VMEM_LIMIT = 56 * 1024 * 1024
MOD_ROWS = 8

TM_IN = 512
TQ = 512
TC_LAT = 512
TM_OUT = 512
TM_MLP = 1024
FC_MLP = 512
TN_ADA = 1024


def _params(n_axes, vmem=V7X_VMEM_LIMIT):
    return pltpu.CompilerParams(dimension_semantics=("arbitrary",) * n_axes,
                                vmem_limit_bytes=vmem)


def _layer_block(shape, layer):
    zeros = (0,) * len(shape)
    return pl.BlockSpec((None,) + tuple(shape), lambda *_: (layer,) + zeros,
                        pipeline_mode=pl.Buffered(1))


def _mod_spec(layer, tm, seq, n_batch):
    return pl.BlockSpec((None, None, 1, 6 * D_MODEL),
                        lambda i, *_: (layer, jnp.minimum(i * tm // seq, n_batch), 0, 0))


def _rms_modulate(x, g, shift, scale):
    ms = jnp.mean(x * x, axis=-1, keepdims=True)
    y = x * lax.rsqrt(ms + EPS) * g
    return y * (1.0 + scale) + shift


def _ada_kernel(c_ref, w_ref, b_ref, o_ref):
    c = c_ref[...]
    s = (c * jax.nn.sigmoid(c)).astype(BF16)
    o_ref[...] = jnp.dot(s, w_ref[...].astype(BF16), preferred_element_type=F32) + b_ref[...]


def _ada_table(cc, w_ada, b_ada):
    n_out = 6 * D_MODEL
    return pl.pallas_call(
        _ada_kernel,
        grid=(DEPTH, n_out // TN_ADA),
        in_specs=[
            pl.BlockSpec((MOD_ROWS, D_MODEL), lambda l, j: (0, 0)),
            pl.BlockSpec((None, D_MODEL, TN_ADA), lambda l, j: (l, 0, j)),
            pl.BlockSpec((None, 1, TN_ADA), lambda l, j: (l, 0, j)),
        ],
        out_specs=pl.BlockSpec((None, MOD_ROWS, TN_ADA), lambda l, j: (l, 0, j)),
        out_shape=jax.ShapeDtypeStruct((DEPTH, MOD_ROWS, n_out), F32),
        compiler_params=_params(2),
        name="ada_table",
    )(cc, w_ada, b_ada.reshape(DEPTH, 1, n_out))


def _inproj_kernel(x_ref, mod_ref, g_ref, cos_ref, sin_ref, w_ref,
                   q_ref, k_ref, v_ref, u_ref, h_s):
    h = _rms_modulate(x_ref[...], g_ref[...],
                      mod_ref[:, 0:D_MODEL], mod_ref[:, D_MODEL:2 * D_MODEL])
    h_s[...] = h.astype(BF16)
    cos = cos_ref[...]
    sin = sin_ref[...]

    def rope(seg):
        return seg * cos + pltpu.roll(seg, HEAD_DIM // 2, 1) * sin

    nq = 512
    for c in range(ATTN_W // nq):
        r = jnp.dot(h_s[...], w_ref[:, c * nq:(c + 1) * nq], preferred_element_type=F32)
        for hh in range(nq // HEAD_DIM):
            lo = hh * HEAD_DIM
            q_ref[:, c * nq + lo:c * nq + lo + HEAD_DIM] = rope(r[:, lo:lo + HEAD_DIM]).astype(BF16)
    r = jnp.dot(h_s[...], w_ref[:, ATTN_W:ATTN_W + 2 * KV_W], preferred_element_type=F32)
    for hh in range(KV_HEADS):
        lo = hh * HEAD_DIM
        k_ref[:, lo:lo + HEAD_DIM] = rope(r[:, lo:lo + HEAD_DIM]).astype(BF16)
    v_ref[...] = r[:, KV_W:2 * KV_W].astype(BF16)
    nu = 256
    u0 = ATTN_W + 2 * KV_W
    for c in range(CONV_CH // nu):
        a = jnp.dot(h_s[...], w_ref[:, u0 + c * nu:u0 + (c + 1) * nu], preferred_element_type=F32)
        gt = jnp.dot(h_s[...], w_ref[:, u0 + CONV_CH + c * nu:u0 + CONV_CH + (c + 1) * nu],
                     preferred_element_type=F32)
        u_ref[:, c * nu:(c + 1) * nu] = a * jax.nn.sigmoid(gt)


def _inproj(xa, mod, g, cos_t, sin_t, w, layer, n_lat_rows, seq):
    rows = xa.shape[0]
    tm = TM_IN
    n_lat = n_lat_rows // tm
    per_seq = seq // tm
    n_batch = n_lat_rows // seq

    def rope_idx(i):
        return (jnp.where(i < n_lat, i % per_seq, per_seq), 0)

    row = lambda i: (i, 0)
    return pl.pallas_call(
        _inproj_kernel,
        grid=(rows // tm,),
        in_specs=[
            pl.BlockSpec((tm, D_MODEL), row),
            _mod_spec(layer, tm, seq, n_batch),
            _layer_block((1, D_MODEL), layer),
            pl.BlockSpec((tm, HEAD_DIM), rope_idx),
            pl.BlockSpec((tm, HEAD_DIM), rope_idx),
            _layer_block((D_MODEL, IN_W), layer),
        ],
        out_specs=[
            pl.BlockSpec((tm, ATTN_W), row),
            pl.BlockSpec((tm, KV_W), row),
            pl.BlockSpec((tm, KV_W), row),
            pl.BlockSpec((tm, CONV_CH), row),
        ],
        out_shape=[
            jax.ShapeDtypeStruct((rows, ATTN_W), BF16),
            jax.ShapeDtypeStruct((rows, KV_W), BF16),
            jax.ShapeDtypeStruct((rows, KV_W), BF16),
            jax.ShapeDtypeStruct((rows, CONV_CH), F32),
        ],
        scratch_shapes=[pltpu.VMEM((tm, D_MODEL), BF16)],
        compiler_params=_params(1),
        name="inproj",
    )(xa, mod, g, cos_t, sin_t, w)


SM_SCALE = HEAD_DIM ** -0.5
EXP2_SCALE = SM_SCALE * math.log2(math.e)


def _nt_dot(a, b):
    return lax.dot_general(a, b, (((1,), (1,)), ((), ())), preferred_element_type=F32)


def _stack_heads(q_ref, r0, nrows, kvh):
    return jnp.concatenate(
        [q_ref[r0:r0 + nrows, (kvh * GQ + g) * HEAD_DIM:(kvh * GQ + g + 1) * HEAD_DIM] for g in range(GQ)],
        axis=0)


def _sink_rows(sink_ref, layer, nrows, kvh):
    return jnp.concatenate(
        [jnp.full((nrows, LANES), sink_ref[layer, kvh * GQ + g] * (1.0 / SM_SCALE), F32)
         for g in range(GQ)], axis=0)


def _softmax_pv(chunks, sink, v_ext):
    mx = chunks[0]
    for ch in chunks[1:]:
        mx = jnp.maximum(mx, ch)
    m = jnp.maximum(jnp.max(mx, axis=-1, keepdims=True), sink)
    e = jnp.concatenate([jnp.exp2((ch - m) * EXP2_SCALE).astype(BF16) for ch in chunks], axis=1)
    o = jnp.dot(e, v_ext, preferred_element_type=F32)
    den = o[:, HEAD_DIM:] + jnp.exp2((sink - m) * EXP2_SCALE)
    return o[:, :HEAD_DIM] / den


def _attn_kernel(sink_ref, q_ref, km_ref, kp_ref, kn_ref, vm_ref, vp_ref, vn_ref, kc_ref, vc_ref,
                 o_ref, kbuf, vbuf, *, layer, n_lat_steps, steps_per_seq, ctx_len):
    i = pl.program_id(0)
    win = 3 * BLOCK
    nblk = TQ // BLOCK

    def store_heads(o, r0, nrows, kvh):
        for g in range(GQ):
            c0 = (kvh * GQ + g) * HEAD_DIM
            o_ref[r0:r0 + nrows, c0:c0 + HEAD_DIM] = o[g * nrows:(g + 1) * nrows].astype(BF16)

    @pl.when(i < n_lat_steps)
    def _latent():
        kbuf[0:BLOCK] = kp_ref[...]
        kbuf[BLOCK:BLOCK + TQ] = km_ref[...]
        kbuf[BLOCK + TQ:2 * BLOCK + TQ] = kn_ref[...]
        vbuf[0:BLOCK] = vp_ref[...]
        vbuf[BLOCK:BLOCK + TQ] = vm_ref[...]
        vbuf[BLOCK + TQ:2 * BLOCK + TQ] = vn_ref[...]
        pos = i % steps_per_seq
        off_first = jnp.where(pos == 0, BLOCK, 0)
        off_last = jnp.where(pos == steps_per_seq - 1, BLOCK, 0)
        r = lax.broadcasted_iota(jnp.int32, (GQ * BLOCK, LANES), 0) & (BLOCK - 1)
        j = lax.broadcasted_iota(jnp.int32, (GQ * BLOCK, LANES), 1)
        ones = jnp.ones((win + ctx_len, HEAD_DIM), BF16)

        def scores(t):
            blk, kvh = divmod(t, KV_HEADS)
            r0, c0 = blk * BLOCK, kvh * HEAD_DIM
            qg = _stack_heads(q_ref, r0, BLOCK, kvh)
            keys = jnp.concatenate([kbuf[r0:r0 + win, c0:c0 + HEAD_DIM], kc_ref[:, c0:c0 + HEAD_DIM]],
                                   axis=0)
            return _nt_dot(qg, keys)

        def finish(t, s):
            blk, kvh = divmod(t, KV_HEADS)
            r0, c0 = blk * BLOCK, kvh * HEAD_DIM
            lo = r + off_first if blk == 0 else r
            hi = r - off_last if blk == nblk - 1 else r
            chunks = [s[:, n * LANES:(n + 1) * LANES] for n in range((win + ctx_len) // LANES)]
            chunks[0] = jnp.where(j >= lo, chunks[0], NEG)
            chunks[2] = jnp.where(j <= hi, chunks[2], NEG)
            vals = jnp.concatenate([vbuf[r0:r0 + win, c0:c0 + HEAD_DIM], vc_ref[:, c0:c0 + HEAD_DIM]],
                                   axis=0)
            v_ext = jnp.concatenate([vals, ones], axis=1)
            o = _softmax_pv(chunks, _sink_rows(sink_ref, layer, BLOCK, kvh), v_ext)
            store_heads(o, r0, BLOCK, kvh)

        n_items = nblk * KV_HEADS
        s_next = scores(0)
        for t in range(n_items):
            s_cur = s_next
            if t + 1 < n_items:
                s_next = scores(t + 1)
            finish(t, s_cur)

    @pl.when(i >= n_lat_steps)
    def _context():
        ones = jnp.ones((ctx_len, HEAD_DIM), BF16)
        for bb in range(TQ // ctx_len):
            r0 = bb * ctx_len
            for kvh in range(KV_HEADS):
                c0 = kvh * HEAD_DIM
                qg = _stack_heads(q_ref, r0, ctx_len, kvh)
                s = _nt_dot(qg, km_ref[r0:r0 + ctx_len, c0:c0 + HEAD_DIM])
                chunks = [s[:, n * LANES:(n + 1) * LANES] for n in range(ctx_len // LANES)]
                v_ext = jnp.concatenate([vm_ref[r0:r0 + ctx_len, c0:c0 + HEAD_DIM], ones], axis=1)
                o = _softmax_pv(chunks, _sink_rows(sink_ref, layer, ctx_len, kvh), v_ext)
                store_heads(o, r0, ctx_len, kvh)


def _attention(sink, q, k, v, layer, n_lat_rows, seq, ctx_len, with_ctx):
    rows = q.shape[0]
    n_lat_steps = n_lat_rows // TQ
    steps = rows // TQ if with_ctx else n_lat_steps
    steps_per_seq = seq // TQ
    n_batch = n_lat_rows // seq
    hb = TQ // BLOCK
    last_blk = rows // BLOCK - 1

    main = lambda i: (i, 0)
    prev = lambda i: (jnp.maximum(i * hb - 1, 0), 0)
    nxt = lambda i: (jnp.minimum((i + 1) * hb, last_blk), 0)
    ctx = lambda i: (n_lat_rows // ctx_len + jnp.minimum(i // steps_per_seq, n_batch - 1), 0)
    kv_specs = [pl.BlockSpec((TQ, KV_W), main), pl.BlockSpec((BLOCK, KV_W), prev),
                pl.BlockSpec((BLOCK, KV_W), nxt)]
    return pl.pallas_call(
        functools.partial(_attn_kernel, layer=layer, n_lat_steps=n_lat_steps,
                          steps_per_seq=steps_per_seq, ctx_len=ctx_len),
        grid=(steps,),
        in_specs=[pl.BlockSpec(memory_space=pltpu.SMEM), pl.BlockSpec((TQ, ATTN_W), main)]
                 + kv_specs + kv_specs
                 + [pl.BlockSpec((ctx_len, KV_W), ctx), pl.BlockSpec((ctx_len, KV_W), ctx)],
        out_specs=pl.BlockSpec((TQ, ATTN_W), main),
        out_shape=jax.ShapeDtypeStruct((steps * TQ, ATTN_W), BF16),
        scratch_shapes=[pltpu.VMEM((TQ + 2 * BLOCK, KV_W), BF16),
                        pltpu.VMEM((TQ + 2 * BLOCK, KV_W), BF16)],
        compiler_params=_params(1),
        name="attention",
    )(sink, q, k, k, k, v, v, v, k, v)


CONV_SHIFT_PAD = 24


def _conv_kernel(u_ref, up_ref, un_ref, w_ref, b_ref, lg_ref, lb_ref, o_ref,
                 ext, ysh, cbuf, *, tc, tiles_per_seq):
    i = pl.program_id(0)
    pos = i % tiles_per_seq
    has_prev = pos != 0
    has_next = pos != tiles_per_seq - 1
    ext[0:CONV_HALO] = jnp.where(has_prev, up_ref[...], 0.0)
    ext[CONV_HALO:CONV_HALO + tc] = u_ref[...]
    ext[CONV_HALO + tc:2 * CONV_HALO + tc] = jnp.where(has_next, un_ref[...], 0.0)
    n_sh = tc + CONV_SHIFT_PAD
    for s in range(SUBLANES):
        ysh[s] = ext[s:s + n_sh, :]
    rc = min(tc, 256)
    first = CONV_HALO - CONV_K // 2
    for cc in range(CONV_CH // LANES):
        c0 = cc * LANES
        for rb in range(tc // rc):
            acc = jnp.zeros((rc, LANES), F32)
            for tap in range(CONV_K):
                q8, s = divmod(first + tap, SUBLANES)
                r0 = SUBLANES * q8 + rb * rc
                acc = acc + ysh[s, r0:r0 + rc, c0:c0 + LANES] * w_ref[tap:tap + 1, c0:c0 + LANES]
            cbuf[rb * rc:(rb + 1) * rc, c0:c0 + LANES] = acc + b_ref[:, c0:c0 + LANES]
    h = cbuf[...]
    mu = jnp.mean(h, axis=-1, keepdims=True)
    d = h - mu
    var = jnp.mean(d * d, axis=-1, keepdims=True)
    y = d * lax.rsqrt(var + EPS) * lg_ref[...] + lb_ref[...]
    o_ref[...] = (y * jax.nn.sigmoid(y)).astype(BF16)


def _drop_ref(kern, pos, *refs):
    return kern(*refs[:pos], *refs[pos + 1:])


def _conv_call(u, prev_out, w, b, lg, lb, layer, tc, row0, n_rows, tiles_per_seq):
    rows = u.shape[0]
    t0 = row0 // tc
    hb = tc // CONV_HALO
    last_h = rows // CONV_HALO - 1
    main = lambda i: (t0 + i, 0)
    prev = lambda i: (jnp.maximum((t0 + i) * hb - 1, 0), 0)
    nxt = lambda i: (jnp.minimum((t0 + i + 1) * hb, last_h), 0)
    in_specs = [
        pl.BlockSpec((tc, CONV_CH), main),
        pl.BlockSpec((CONV_HALO, CONV_CH), prev),
        pl.BlockSpec((CONV_HALO, CONV_CH), nxt),
        _layer_block((CONV_K, CONV_CH), layer),
        _layer_block((1, CONV_CH), layer),
        _layer_block((1, CONV_CH), layer),
        _layer_block((1, CONV_CH), layer),
    ]
    args = [u, u, u, w, b, lg, lb]
    kern = functools.partial(_conv_kernel, tc=tc, tiles_per_seq=tiles_per_seq)
    aliases = {}
    if prev_out is not None:
        in_specs.append(pl.BlockSpec(memory_space=pl.ANY))
        args.append(prev_out)
        aliases = {len(args) - 1: 0}
        kern = functools.partial(_drop_ref, kern, len(args) - 1)
    n_sh = tc + CONV_SHIFT_PAD
    return pl.pallas_call(
        kern,
        grid=(n_rows // tc,),
        in_specs=in_specs,
        out_specs=pl.BlockSpec((tc, CONV_CH), main),
        out_shape=jax.ShapeDtypeStruct((rows, CONV_CH), BF16),
        scratch_shapes=[pltpu.VMEM((tc + 2 * CONV_HALO, CONV_CH), F32),
                        pltpu.VMEM((SUBLANES, n_sh, CONV_CH), F32),
                        pltpu.VMEM((tc, CONV_CH), F32)],
        input_output_aliases=aliases,
        compiler_params=_params(1),
        name="conv_ctx" if prev_out is not None else "conv_lat",
    )(*args)


def _outproj_kernel(att_ref, cv_ref, x_ref, mod_ref, w_ref, o_ref):
    mix = jnp.concatenate([att_ref[...], cv_ref[...]], axis=1)
    nn = 512
    for c in range(D_MODEL // nn):
        r = jnp.dot(mix, w_ref[:, c * nn:(c + 1) * nn], preferred_element_type=F32)
        g1 = mod_ref[:, 2 * D_MODEL + c * nn:2 * D_MODEL + (c + 1) * nn]
        o_ref[:, c * nn:(c + 1) * nn] = x_ref[:, c * nn:(c + 1) * nn] + g1 * r


def _outproj(att, cv, xa, mod, w, layer, n_rows, n_lat_rows, seq):
    tm = TM_OUT
    n_batch = n_lat_rows // seq
    row = lambda i: (i, 0)
    alias = {2: 0} if n_rows == xa.shape[0] else {}
    return pl.pallas_call(
        _outproj_kernel,
        grid=(n_rows // tm,),
        in_specs=[
            pl.BlockSpec((tm, ATTN_W), row),
            pl.BlockSpec((tm, CONV_CH), row),
            pl.BlockSpec((tm, D_MODEL), row),
            _mod_spec(layer, tm, seq, n_batch),
            _layer_block((MIX_W, D_MODEL), layer),
        ],
        out_specs=pl.BlockSpec((tm, D_MODEL), row),
        out_shape=jax.ShapeDtypeStruct((n_rows, D_MODEL), F32),
        input_output_aliases=alias,
        compiler_params=_params(1),
        name="outproj",
    )(att, cv, xa, mod, w)


def _mlp_kernel(x_ref, mod_ref, g_ref, gf_ref, w1_ref, w2_ref, o_ref, h_s, *, final):
    j = pl.program_id(1)

    @pl.when(j == 0)
    def _prologue():
        h = _rms_modulate(x_ref[...], g_ref[...],
                          mod_ref[:, 3 * D_MODEL:4 * D_MODEL], mod_ref[:, 4 * D_MODEL:5 * D_MODEL])
        h_s[...] = h.astype(BF16)
        o_ref[...] = jnp.zeros_like(o_ref)

    a = jnp.dot(h_s[...], w1_ref[...], preferred_element_type=F32)
    a = jnp.maximum(a, 0.0)
    o_ref[...] += jnp.dot((a * a).astype(BF16), w2_ref[...], preferred_element_type=F32)

    @pl.when(j == pl.num_programs(1) - 1)
    def _epilogue():
        y = x_ref[...] + mod_ref[:, 5 * D_MODEL:6 * D_MODEL] * o_ref[...]
        if final:
            ms = jnp.mean(y * y, axis=-1, keepdims=True)
            y = y * lax.rsqrt(ms + EPS) * gf_ref[...]
        o_ref[...] = y


def _mlp(xa, mod, g, gf, w1, w2, layer, n_lat_rows, seq, final):
    rows = xa.shape[0]
    tm, fc = TM_MLP, FC_MLP
    n_batch = n_lat_rows // seq
    row = lambda i, j: (i, 0)
    return pl.pallas_call(
        functools.partial(_mlp_kernel, final=final),
        grid=(rows // tm, D_FF // fc),
        in_specs=[
            pl.BlockSpec((tm, D_MODEL), row),
            _mod_spec(layer, tm, seq, n_batch),
            _layer_block((1, D_MODEL), layer),
            pl.BlockSpec((1, D_MODEL), lambda i, j: (0, 0)),
            pl.BlockSpec((None, D_MODEL, fc), lambda i, j: (layer, 0, j)),
            pl.BlockSpec((None, fc, D_MODEL), lambda i, j: (layer, j, 0)),
        ],
        out_specs=pl.BlockSpec((tm, D_MODEL), row),
        out_shape=jax.ShapeDtypeStruct((rows, D_MODEL), F32),
        scratch_shapes=[pltpu.VMEM((tm, D_MODEL), BF16)],
        input_output_aliases={0: 0},
        compiler_params=_params(2),
        name="mlp",
    )(xa, mod, g, gf, w1, w2)


def _rope_tables(seq, pad_rows):
    t = jnp.arange(seq, dtype=jnp.int32)
    row = (t // GRID_W).astype(F32)
    col = (t % GRID_W).astype(F32)
    n_freq = HEAD_DIM // 4
    inv = ROPE_BASE ** (-jnp.arange(n_freq, dtype=F32) / n_freq)
    theta = jnp.concatenate([row[:, None] * inv, col[:, None] * inv], axis=-1)
    theta = jnp.concatenate([theta, theta], axis=-1)
    sign = jnp.concatenate([-jnp.ones((HEAD_DIM // 2,), F32), jnp.ones((HEAD_DIM // 2,), F32)])
    cos = jnp.concatenate([jnp.cos(theta), jnp.ones((pad_rows, HEAD_DIM), F32)], axis=0)
    sin = jnp.concatenate([jnp.sin(theta) * sign, jnp.zeros((pad_rows, HEAD_DIM), F32)], axis=0)
    return cos, sin


def kernel(x, c, ctx, c_ctx, w_ada, b_ada, g_mix, g_mlp, w_in, attn_sink, conv_w, conv_b,
           conv_ln_g, conv_ln_b, w_out, w_mlp1, w_mlp2, g_final):
    n_batch, seq, d = x.shape
    ctx_len = ctx.shape[1]
    n_lat_rows = n_batch * seq
    assert d == D_MODEL and n_batch + 1 <= MOD_ROWS
    assert seq % TM_MLP == 0 and (n_batch * ctx_len) % TM_MLP == 0 and TQ % ctx_len == 0

    xa = jnp.concatenate([x.reshape(n_lat_rows, d), ctx.reshape(n_batch * ctx_len, d)], axis=0)
    cc = jnp.concatenate([c, c_ctx[None], jnp.zeros((MOD_ROWS - n_batch - 1, d), F32)], axis=0)
    mod = _ada_table(cc, w_ada, b_ada).reshape(DEPTH, MOD_ROWS, 1, 6 * d)
    cos_t, sin_t = _rope_tables(seq, TM_IN)

    w_in_b = w_in.astype(BF16)
    w_out_b = w_out.astype(BF16)
    w1_b = w_mlp1.astype(BF16)
    w2_b = w_mlp2.astype(BF16)
    g_mix3 = g_mix.reshape(DEPTH, 1, d)
    g_mlp3 = g_mlp.reshape(DEPTH, 1, d)
    gf = g_final.reshape(1, d)
    conv_b3 = conv_b.reshape(DEPTH, 1, CONV_CH)
    conv_lg3 = conv_ln_g.reshape(DEPTH, 1, CONV_CH)
    conv_lb3 = conv_ln_b.reshape(DEPTH, 1, CONV_CH)

    for l in range(DEPTH):
        last = l == DEPTH - 1
        q, k, v, u = _inproj(xa, mod, g_mix3, cos_t, sin_t, w_in_b, l, n_lat_rows, seq)
        att = _attention(attn_sink, q, k, v, l, n_lat_rows, seq, ctx_len, with_ctx=not last)
        conv_args = (conv_w, conv_b3, conv_lg3, conv_lb3, l)
        cv = _conv_call(u, None, *conv_args, TC_LAT, 0, n_lat_rows, seq // TC_LAT)
        if not last:
            cv = _conv_call(u, cv, *conv_args, ctx_len, n_lat_rows, n_batch * ctx_len, 1)
        n_rows = n_lat_rows if last else xa.shape[0]
        xa = _outproj(att, cv, xa, mod, w_out_b, l, n_rows, n_lat_rows, seq)
        xa = _mlp(xa, mod, g_mlp3, gf, w1_b, w2_b, l, n_lat_rows, seq, last)
    return xa.reshape(n_batch, seq, d)
```

```python
import functools
import math

import jax
import jax.numpy as jnp
from jax import lax
from jax.experimental import pallas as pl
from jax.experimental.pallas import tpu as pltpu

F32 = jnp.float32
BF16 = jnp.bfloat16

D_MODEL = 2048
DEPTH = 4
GRID_W = 64
HEAD_DIM = 128
ATTN_HEADS = 8
KV_HEADS = 2
GQ = ATTN_HEADS // KV_HEADS
ATTN_W = ATTN_HEADS * HEAD_DIM
KV_W = KV_HEADS * HEAD_DIM
BLOCK = 128
ROPE_BASE = 10000.0
CONV_CH = D_MODEL // 2
CONV_K = 31
CONV_HALO = 16
MIX_W = ATTN_W + CONV_CH
IN_W = ATTN_W + 2 * KV_W + 2 * CONV_CH
D_FF = 4 * D_MODEL
EPS = 1e-6
NEG = -1e30

SUBLANES = 8
LANES = 128
V7X_VMEM_LIMIT = 56 * 1024 * 1024
MOD_ROWS = 8

TM_IN = 512
TQ = 512
TM_OUT = 512
TM_MLP = 1024
FC_MLP = 512
TN_ADA = 1024


def _params(n_axes, vmem=V7X_VMEM_LIMIT):
    return pltpu.CompilerParams(dimension_semantics=("arbitrary",) * n_axes,
                                vmem_limit_bytes=vmem)


def _layer_block(shape, layer):
    zeros = (0,) * len(shape)
    return pl.BlockSpec((None,) + tuple(shape), lambda *_: (layer,) + zeros,
                        pipeline_mode=pl.Buffered(1))


def _mod_spec(layer, tm, seq, n_batch, t0=0):
    return pl.BlockSpec((None, None, 1, 6 * D_MODEL),
                        lambda i, *_: (layer, jnp.minimum((t0 + i) * tm // seq, n_batch), 0, 0))


def _rms_modulate(x, g, shift, scale):
    ms = jnp.mean(x * x, axis=-1, keepdims=True)
    y = x * lax.rsqrt(ms + EPS) * g
    return y * (1.0 + scale) + shift


def _ada_kernel(c_ref, w_ref, b_ref, o_ref):
    c = c_ref[...]
    s = (c * jax.nn.sigmoid(c)).astype(BF16)
    o_ref[...] = jnp.dot(s, w_ref[...].astype(BF16), preferred_element_type=F32) + b_ref[...]


def _ada_table(cc, w_ada, b_ada):
    n_out = 6 * D_MODEL
    return pl.pallas_call(
        _ada_kernel,
        grid=(DEPTH, n_out // TN_ADA),
        in_specs=[
            pl.BlockSpec((MOD_ROWS, D_MODEL), lambda l, j: (0, 0)),
            pl.BlockSpec((None, D_MODEL, TN_ADA), lambda l, j: (l, 0, j)),
            pl.BlockSpec((None, 1, TN_ADA), lambda l, j: (l, 0, j)),
        ],
        out_specs=pl.BlockSpec((None, MOD_ROWS, TN_ADA), lambda l, j: (l, 0, j)),
        out_shape=jax.ShapeDtypeStruct((DEPTH, MOD_ROWS, n_out), F32),
        compiler_params=_params(2),
        name="ada_table",
    )(cc, w_ada, b_ada.reshape(DEPTH, 1, n_out))


def _inproj_kernel(*refs, n_lat_tiles, split_src):
    if split_src:
        xl_ref, xc_ref, mod_ref, g_ref, cos_ref, sin_ref, w_ref, q_ref, k_ref, v_ref, u_ref, xa_ref, h_s = refs
        x = jnp.where(pl.program_id(0) < n_lat_tiles, xl_ref[...], xc_ref[...])
        xa_ref[...] = x
    else:
        x_ref, mod_ref, g_ref, cos_ref, sin_ref, w_ref, q_ref, k_ref, v_ref, u_ref, h_s = refs
        x = x_ref[...]
    h = _rms_modulate(x, g_ref[...], mod_ref[:, 0:D_MODEL], mod_ref[:, D_MODEL:2 * D_MODEL])
    h_s[...] = h.astype(BF16)
    cos = cos_ref[...]
    sin = sin_ref[...]

    def rope(seg):
        return seg * cos + pltpu.roll(seg, HEAD_DIM // 2, 1) * sin

    nq = 512
    for c in range(ATTN_W // nq):
        r = jnp.dot(h_s[...], w_ref[:, c * nq:(c + 1) * nq], preferred_element_type=F32)
        for hh in range(nq // HEAD_DIM):
            lo = hh * HEAD_DIM
            q_ref[:, c * nq + lo:c * nq + lo + HEAD_DIM] = rope(r[:, lo:lo + HEAD_DIM]).astype(BF16)
    r = jnp.dot(h_s[...], w_ref[:, ATTN_W:ATTN_W + 2 * KV_W], preferred_element_type=F32)
    for hh in range(KV_HEADS):
        lo = hh * HEAD_DIM
        k_ref[:, lo:lo + HEAD_DIM] = rope(r[:, lo:lo + HEAD_DIM]).astype(BF16)
    v_ref[...] = r[:, KV_W:2 * KV_W].astype(BF16)
    nu = 256
    u0 = ATTN_W + 2 * KV_W
    for c in range(CONV_CH // nu):
        a = jnp.dot(h_s[...], w_ref[:, u0 + c * nu:u0 + (c + 1) * nu], preferred_element_type=F32)
        gt = jnp.dot(h_s[...], w_ref[:, u0 + CONV_CH + c * nu:u0 + CONV_CH + (c + 1) * nu],
                     preferred_element_type=F32)
        u_ref[:, c * nu:(c + 1) * nu] = a * jax.nn.sigmoid(gt)


def _inproj(srcs, mod, g, cos_t, sin_t, w, layer, rows, n_lat_rows, seq):
    tm = TM_IN
    n_lat = n_lat_rows // tm
    per_seq = seq // tm
    n_batch = n_lat_rows // seq
    split_src = len(srcs) == 2

    def rope_idx(i):
        return (jnp.where(i < n_lat, i % per_seq, per_seq), 0)

    row = lambda i: (i, 0)
    if split_src:
        src_specs = [pl.BlockSpec((tm, D_MODEL), lambda i: (jnp.minimum(i, n_lat - 1), 0)),
                     pl.BlockSpec((tm, D_MODEL), lambda i: (jnp.maximum(i - n_lat, 0), 0))]
    else:
        src_specs = [pl.BlockSpec((tm, D_MODEL), row)]
    out_specs = [pl.BlockSpec((tm, ATTN_W), row), pl.BlockSpec((tm, KV_W), row),
                 pl.BlockSpec((tm, KV_W), row), pl.BlockSpec((tm, CONV_CH), row)]
    out_shape = [jax.ShapeDtypeStruct((rows, ATTN_W), BF16), jax.ShapeDtypeStruct((rows, KV_W), BF16),
                 jax.ShapeDtypeStruct((rows, KV_W), BF16), jax.ShapeDtypeStruct((rows, CONV_CH), F32)]
    if split_src:
        out_specs.append(pl.BlockSpec((tm, D_MODEL), row))
        out_shape.append(jax.ShapeDtypeStruct((rows, D_MODEL), F32))
    return pl.pallas_call(
        functools.partial(_inproj_kernel, n_lat_tiles=n_lat, split_src=split_src),
        grid=(rows // tm,),
        in_specs=src_specs + [
            _mod_spec(layer, tm, seq, n_batch),
            _layer_block((1, D_MODEL), layer),
            pl.BlockSpec((tm, HEAD_DIM), rope_idx),
            pl.BlockSpec((tm, HEAD_DIM), rope_idx),
            _layer_block((D_MODEL, IN_W), layer),
        ],
        out_specs=out_specs,
        out_shape=out_shape,
        scratch_shapes=[pltpu.VMEM((tm, D_MODEL), BF16)],
        compiler_params=_params(1),
        name="inproj",
    )(*srcs, mod, g, cos_t, sin_t, w)


SM_SCALE = HEAD_DIM ** -0.5
EXP2_SCALE = SM_SCALE * math.log2(math.e)


def _nt_dot(a, b):
    return lax.dot_general(a, b, (((1,), (1,)), ((), ())), preferred_element_type=F32)


def _stack_heads(q_ref, r0, nrows, kvh):
    return jnp.concatenate(
        [q_ref[r0:r0 + nrows, (kvh * GQ + g) * HEAD_DIM:(kvh * GQ + g + 1) * HEAD_DIM] for g in range(GQ)],
        axis=0)


def _sink_rows(sink_ref, layer, nrows, kvh):
    return jnp.concatenate(
        [jnp.full((nrows, LANES), sink_ref[layer, kvh * GQ + g] * (1.0 / SM_SCALE), F32)
         for g in range(GQ)], axis=0)


def _softmax_pv(chunks, sink, v_ext):
    mx = chunks[0]
    for ch in chunks[1:]:
        mx = jnp.maximum(mx, ch)
    m = jnp.maximum(jnp.max(mx, axis=-1, keepdims=True), sink)
    e = jnp.concatenate([jnp.exp2((ch - m) * EXP2_SCALE).astype(BF16) for ch in chunks], axis=1)
    o = jnp.dot(e, v_ext, preferred_element_type=F32)
    den = o[:, HEAD_DIM:] + jnp.exp2((sink - m) * EXP2_SCALE)
    return o[:, :HEAD_DIM] / den


def _attn_kernel(sink_ref, q_ref, km_ref, kp_ref, kn_ref, vm_ref, vp_ref, vn_ref, kc_ref, vc_ref,
                 o_ref, kbuf, vbuf, *, layer, n_lat_steps, steps_per_seq, ctx_len):
    i = pl.program_id(0)
    win = 3 * BLOCK
    nblk = TQ // BLOCK

    def store_heads(o, r0, nrows, kvh):
        for g in range(GQ):
            c0 = (kvh * GQ + g) * HEAD_DIM
            o_ref[r0:r0 + nrows, c0:c0 + HEAD_DIM] = o[g * nrows:(g + 1) * nrows].astype(BF16)

    @pl.when(i < n_lat_steps)
    def _latent():
        kbuf[0:BLOCK] = kp_ref[...]
        kbuf[BLOCK:BLOCK + TQ] = km_ref[...]
        kbuf[BLOCK + TQ:2 * BLOCK + TQ] = kn_ref[...]
        vbuf[0:BLOCK] = vp_ref[...]
        vbuf[BLOCK:BLOCK + TQ] = vm_ref[...]
        vbuf[BLOCK + TQ:2 * BLOCK + TQ] = vn_ref[...]
        pos = i % steps_per_seq
        off_first = jnp.where(pos == 0, BLOCK, 0)
        off_last = jnp.where(pos == steps_per_seq - 1, BLOCK, 0)
        r = lax.broadcasted_iota(jnp.int32, (GQ * BLOCK, LANES), 0) & (BLOCK - 1)
        j = lax.broadcasted_iota(jnp.int32, (GQ * BLOCK, LANES), 1)
        ones = jnp.ones((win + ctx_len, HEAD_DIM), BF16)

        def scores(t):
            blk, kvh = divmod(t, KV_HEADS)
            r0, c0 = blk * BLOCK, kvh * HEAD_DIM
            qg = _stack_heads(q_ref, r0, BLOCK, kvh)
            keys = jnp.concatenate([kbuf[r0:r0 + win, c0:c0 + HEAD_DIM], kc_ref[:, c0:c0 + HEAD_DIM]],
                                   axis=0)
            return _nt_dot(qg, keys)

        def finish(t, s):
            blk, kvh = divmod(t, KV_HEADS)
            r0, c0 = blk * BLOCK, kvh * HEAD_DIM
            lo = r + off_first if blk == 0 else r
            hi = r - off_last if blk == nblk - 1 else r
            chunks = [s[:, n * LANES:(n + 1) * LANES] for n in range((win + ctx_len) // LANES)]
            chunks[0] = jnp.where(j >= lo, chunks[0], NEG)
            chunks[2] = jnp.where(j <= hi, chunks[2], NEG)
            vals = jnp.concatenate([vbuf[r0:r0 + win, c0:c0 + HEAD_DIM], vc_ref[:, c0:c0 + HEAD_DIM]],
                                   axis=0)
            v_ext = jnp.concatenate([vals, ones], axis=1)
            o = _softmax_pv(chunks, _sink_rows(sink_ref, layer, BLOCK, kvh), v_ext)
            store_heads(o, r0, BLOCK, kvh)

        n_items = nblk * KV_HEADS
        s_next = scores(0)
        for t in range(n_items):
            s_cur = s_next
            if t + 1 < n_items:
                s_next = scores(t + 1)
            finish(t, s_cur)

    @pl.when(i >= n_lat_steps)
    def _context():
        ones = jnp.ones((ctx_len, HEAD_DIM), BF16)
        for bb in range(TQ // ctx_len):
            r0 = bb * ctx_len
            for kvh in range(KV_HEADS):
                c0 = kvh * HEAD_DIM
                qg = _stack_heads(q_ref, r0, ctx_len, kvh)
                s = _nt_dot(qg, km_ref[r0:r0 + ctx_len, c0:c0 + HEAD_DIM])
                chunks = [s[:, n * LANES:(n + 1) * LANES] for n in range(ctx_len // LANES)]
                v_ext = jnp.concatenate([vm_ref[r0:r0 + ctx_len, c0:c0 + HEAD_DIM], ones], axis=1)
                o = _softmax_pv(chunks, _sink_rows(sink_ref, layer, ctx_len, kvh), v_ext)
                store_heads(o, r0, ctx_len, kvh)


def _attention(sink, q, k, v, layer, n_lat_rows, seq, ctx_len, with_ctx):
    rows = q.shape[0]
    n_lat_steps = n_lat_rows // TQ
    steps = rows // TQ if with_ctx else n_lat_steps
    steps_per_seq = seq // TQ
    n_batch = n_lat_rows // seq
    hb = TQ // BLOCK
    last_blk = rows // BLOCK - 1

    main = lambda i: (i, 0)
    prev = lambda i: (jnp.maximum(i * hb - 1, 0), 0)
    nxt = lambda i: (jnp.minimum((i + 1) * hb, last_blk), 0)
    ctx = lambda i: (n_lat_rows // ctx_len + jnp.minimum(i // steps_per_seq, n_batch - 1), 0)
    kv_specs = [pl.BlockSpec((TQ, KV_W), main), pl.BlockSpec((BLOCK, KV_W), prev),
                pl.BlockSpec((BLOCK, KV_W), nxt)]
    return pl.pallas_call(
        functools.partial(_attn_kernel, layer=layer, n_lat_steps=n_lat_steps,
                          steps_per_seq=steps_per_seq, ctx_len=ctx_len),
        grid=(steps,),
        in_specs=[pl.BlockSpec(memory_space=pltpu.SMEM), pl.BlockSpec((TQ, ATTN_W), main)]
                 + kv_specs + kv_specs
                 + [pl.BlockSpec((ctx_len, KV_W), ctx), pl.BlockSpec((ctx_len, KV_W), ctx)],
        out_specs=pl.BlockSpec((TQ, ATTN_W), main),
        out_shape=jax.ShapeDtypeStruct((steps * TQ, ATTN_W), BF16),
        scratch_shapes=[pltpu.VMEM((TQ + 2 * BLOCK, KV_W), BF16),
                        pltpu.VMEM((TQ + 2 * BLOCK, KV_W), BF16)],
        compiler_params=_params(1),
        name="attention",
    )(sink, q, k, k, k, v, v, v, k, v)


CONV_SHIFT_PAD = 24


CONV_HALF = CONV_CH // 2
CONV_ROWS = 128


def _conv_tile(u_ref, up_ref, un_ref, has_prev, has_next, w_ref, b_ref, lg_ref, lb_ref,
               ext, ysh, cbuf, tc, side_work=()):
    if up_ref is None:
        ext[0:CONV_HALO] = jnp.zeros((CONV_HALO, CONV_CH), F32)
    else:
        ext[0:CONV_HALO] = jnp.where(has_prev, up_ref[...], 0.0)
    ext[CONV_HALO:CONV_HALO + tc] = u_ref[...]
    ext[CONV_HALO + tc:2 * CONV_HALO + tc] = jnp.where(has_next, un_ref[...], 0.0)
    n_sh = tc + CONV_SHIFT_PAD
    rc = min(tc, CONV_ROWS)
    first = CONV_HALO - CONV_K // 2
    side = list(side_work)
    for half in range(CONV_CH // CONV_HALF):
        h0 = half * CONV_HALF
        for s in range(1, SUBLANES):
            ysh[s - 1] = ext[s:s + n_sh, h0:h0 + CONV_HALF]
        for cc in range(CONV_HALF // LANES):
            l0 = cc * LANES
            c0 = h0 + l0
            for rb in range(tc // rc):
                acc = jnp.zeros((rc, LANES), F32)
                for tap in range(CONV_K):
                    q8, s = divmod(first + tap, SUBLANES)
                    r0 = SUBLANES * q8 + rb * rc
                    if s == 0:
                        rows = ext[r0:r0 + rc, c0:c0 + LANES]
                    else:
                        rows = ysh[s - 1, r0:r0 + rc, l0:l0 + LANES]
                    acc = acc + rows * w_ref[tap:tap + 1, c0:c0 + LANES]
                cbuf[rb * rc:(rb + 1) * rc, c0:c0 + LANES] = acc + b_ref[:, c0:c0 + LANES]
            if side:
                side.pop(0)()
    for work in side:
        work()
    h = cbuf[...]
    mu = jnp.mean(h, axis=-1, keepdims=True)
    d = h - mu
    var = jnp.mean(d * d, axis=-1, keepdims=True)
    y = d * lax.rsqrt(var + EPS) * lg_ref[...] + lb_ref[...]
    return (y * jax.nn.sigmoid(y)).astype(BF16)


def _conv_scratch(tc):
    return [pltpu.VMEM((tc + 2 * CONV_HALO, CONV_CH), F32),
            pltpu.VMEM((SUBLANES - 1, tc + CONV_SHIFT_PAD, CONV_HALF), F32),
            pltpu.VMEM((tc, CONV_CH), F32)]


def _conv_kernel(u_ref, up_ref, un_ref, w_ref, b_ref, lg_ref, lb_ref, o_ref,
                 ext, ysh, cbuf, *, tc, tiles_per_seq):
    pos = pl.program_id(0) % tiles_per_seq
    o_ref[...] = _conv_tile(u_ref, up_ref, un_ref, pos != 0, pos != tiles_per_seq - 1,
                            w_ref, b_ref, lg_ref, lb_ref, ext, ysh, cbuf, tc)


def _conv_call(u, w, b, lg, lb, layer, tc, row0, n_rows, tiles_per_seq):
    rows = u.shape[0]
    t0 = row0 // tc
    hb = tc // CONV_HALO
    last_h = rows // CONV_HALO - 1
    return pl.pallas_call(
        functools.partial(_conv_kernel, tc=tc, tiles_per_seq=tiles_per_seq),
        grid=(n_rows // tc,),
        in_specs=[
            pl.BlockSpec((tc, CONV_CH), lambda i: (t0 + i, 0)),
            pl.BlockSpec((CONV_HALO, CONV_CH), lambda i: (jnp.maximum((t0 + i) * hb - 1, 0), 0)),
            pl.BlockSpec((CONV_HALO, CONV_CH), lambda i: (jnp.minimum((t0 + i + 1) * hb, last_h), 0)),
            _layer_block((CONV_K, CONV_CH), layer),
            _layer_block((1, CONV_CH), layer),
            _layer_block((1, CONV_CH), layer),
            _layer_block((1, CONV_CH), layer),
        ],
        out_specs=pl.BlockSpec((tc, CONV_CH), lambda i: (i, 0)),
        out_shape=jax.ShapeDtypeStruct((n_rows, CONV_CH), BF16),
        scratch_shapes=_conv_scratch(tc),
        compiler_params=_params(1),
        name="conv",
    )(u, u, u, w, b, lg, lb)


OUT_CHUNK = 256


def _outproj_chunk(c, att_ref, load_cv, x_ref, mod_ref, w_ref, o_ref):
    cols = slice(c * OUT_CHUNK, (c + 1) * OUT_CHUNK)
    mix = jnp.concatenate([att_ref[...], load_cv()], axis=1)
    r = jnp.dot(mix, w_ref[:, cols], preferred_element_type=F32)
    g1 = mod_ref[:, 2 * D_MODEL + c * OUT_CHUNK:2 * D_MODEL + (c + 1) * OUT_CHUNK]
    o_ref[:, cols] = x_ref[:, cols] + g1 * r


def _outproj_kernel(att_ref, cv_ref, x_ref, mod_ref, w_ref, o_ref):
    for c in range(D_MODEL // OUT_CHUNK):
        _outproj_chunk(c, att_ref, lambda: cv_ref[...], x_ref, mod_ref, w_ref, o_ref)


def _outproj(att, cv, xa, row0, n_rows, mod, w, layer, n_lat_rows, seq):
    tm = TM_OUT
    n_batch = n_lat_rows // seq
    t0 = row0 // tm
    row = lambda i: (t0 + i, 0)
    return pl.pallas_call(
        _outproj_kernel,
        grid=(n_rows // tm,),
        in_specs=[
            pl.BlockSpec((tm, ATTN_W), row),
            pl.BlockSpec((tm, CONV_CH), lambda i: (i, 0)),
            pl.BlockSpec((tm, D_MODEL), row),
            _mod_spec(layer, tm, seq, n_batch, t0),
            _layer_block((MIX_W, D_MODEL), layer),
        ],
        out_specs=pl.BlockSpec((tm, D_MODEL), row),
        out_shape=jax.ShapeDtypeStruct(xa.shape, F32),
        input_output_aliases={2: 0},
        compiler_params=_params(1),
        name="outproj",
    )(att, cv, xa, mod, w)


def _outproj_conv_kernel(att_ref, x_ref, mod_ref, w_ref, u0_ref, u0n_ref, un_ref, unp_ref, unn_ref,
                         cw_ref, cb_ref, lg_ref, lb_ref, o_ref, cvbuf, ext, ysh, cbuf,
                         *, tc, tiles_per_seq):
    i = pl.program_id(0)
    conv_refs = (cw_ref, cb_ref, lg_ref, lb_ref, ext, ysh, cbuf, tc)

    @pl.when(i == 0)
    def _first():
        cvbuf[0] = _conv_tile(u0_ref, None, u0n_ref, False, tiles_per_seq > 1, *conv_refs)

    nxt = jnp.minimum(i + 1, pl.num_programs(0) - 1)
    pos = nxt % tiles_per_seq
    slot = i % 2
    chunks = [functools.partial(_outproj_chunk, c, att_ref, lambda: cvbuf[slot], x_ref, mod_ref, w_ref, o_ref)
              for c in range(D_MODEL // OUT_CHUNK)]
    cvbuf[1 - slot] = _conv_tile(un_ref, unp_ref, unn_ref, pos != 0, pos != tiles_per_seq - 1,
                                 *conv_refs, side_work=chunks)


def _outproj_conv(att, u, res, mod, w, cw, cb, lg, lb, layer, out_rows, n_lat_rows, seq):
    tm = TM_OUT
    n_tiles = n_lat_rows // tm
    tiles_per_seq = seq // tm
    n_batch = n_lat_rows // seq
    hb = tm // CONV_HALO
    row = lambda i: (i, 0)
    nxt = lambda i: jnp.minimum(i + 1, n_tiles - 1)
    const = dict(pipeline_mode=pl.Buffered(1))
    return pl.pallas_call(
        functools.partial(_outproj_conv_kernel, tc=tm, tiles_per_seq=tiles_per_seq),
        grid=(n_tiles,),
        in_specs=[
            pl.BlockSpec((tm, ATTN_W), row),
            pl.BlockSpec((tm, D_MODEL), row),
            _mod_spec(layer, tm, seq, n_batch),
            _layer_block((MIX_W, D_MODEL), layer),
            pl.BlockSpec((tm, CONV_CH), lambda i: (0, 0), **const),
            pl.BlockSpec((CONV_HALO, CONV_CH), lambda i: (hb, 0), **const),
            pl.BlockSpec((tm, CONV_CH), lambda i: (nxt(i), 0)),
            pl.BlockSpec((CONV_HALO, CONV_CH), lambda i: (nxt(i) * hb - 1, 0)),
            pl.BlockSpec((CONV_HALO, CONV_CH), lambda i: ((nxt(i) + 1) * hb, 0)),
            _layer_block((CONV_K, CONV_CH), layer),
            _layer_block((1, CONV_CH), layer),
            _layer_block((1, CONV_CH), layer),
            _layer_block((1, CONV_CH), layer),
        ],
        out_specs=pl.BlockSpec((tm, D_MODEL), row),
        out_shape=jax.ShapeDtypeStruct((out_rows, D_MODEL), F32),
        scratch_shapes=[pltpu.VMEM((2, tm, CONV_CH), BF16)] + _conv_scratch(tm),
        input_output_aliases={1: 0} if res.shape[0] == out_rows else {},
        compiler_params=_params(1),
        name="outproj_conv",
    )(att, res, mod, w, u, u, u, u, u, cw, cb, lg, lb)


def _mlp_kernel(x_ref, mod_ref, g_ref, gf_ref, w1_ref, w2_ref, o_ref, h_s, *, final):
    j = pl.program_id(1)

    @pl.when(j == 0)
    def _prologue():
        h = _rms_modulate(x_ref[...], g_ref[...],
                          mod_ref[:, 3 * D_MODEL:4 * D_MODEL], mod_ref[:, 4 * D_MODEL:5 * D_MODEL])
        h_s[...] = h.astype(BF16)
        o_ref[...] = jnp.zeros_like(o_ref)

    a = jnp.dot(h_s[...], w1_ref[...], preferred_element_type=F32)
    a = jnp.maximum(a, 0.0)
    o_ref[...] += jnp.dot((a * a).astype(BF16), w2_ref[...], preferred_element_type=F32)

    @pl.when(j == pl.num_programs(1) - 1)
    def _epilogue():
        y = x_ref[...] + mod_ref[:, 5 * D_MODEL:6 * D_MODEL] * o_ref[...]
        if final:
            ms = jnp.mean(y * y, axis=-1, keepdims=True)
            y = y * lax.rsqrt(ms + EPS) * gf_ref[...]
        o_ref[...] = y


def _mlp(xa, mod, g, gf, w1, w2, layer, n_lat_rows, seq, final):
    rows = xa.shape[0]
    tm, fc = TM_MLP, FC_MLP
    n_batch = n_lat_rows // seq
    row = lambda i, j: (i, 0)
    return pl.pallas_call(
        functools.partial(_mlp_kernel, final=final),
        grid=(rows // tm, D_FF // fc),
        in_specs=[
            pl.BlockSpec((tm, D_MODEL), row),
            _mod_spec(layer, tm, seq, n_batch),
            _layer_block((1, D_MODEL), layer),
            pl.BlockSpec((1, D_MODEL), lambda i, j: (0, 0)),
            pl.BlockSpec((None, D_MODEL, fc), lambda i, j: (layer, 0, j)),
            pl.BlockSpec((None, fc, D_MODEL), lambda i, j: (layer, j, 0)),
        ],
        out_specs=pl.BlockSpec((tm, D_MODEL), row),
        out_shape=jax.ShapeDtypeStruct((rows, D_MODEL), F32),
        scratch_shapes=[pltpu.VMEM((tm, D_MODEL), BF16)],
        input_output_aliases={0: 0},
        compiler_params=_params(2),
        name="mlp",
    )(xa, mod, g, gf, w1, w2)


def _rope_tables(seq, pad_rows):
    t = jnp.arange(seq, dtype=jnp.int32)
    row = (t // GRID_W).astype(F32)
    col = (t % GRID_W).astype(F32)
    n_freq = HEAD_DIM // 4
    inv = ROPE_BASE ** (-jnp.arange(n_freq, dtype=F32) / n_freq)
    theta = jnp.concatenate([row[:, None] * inv, col[:, None] * inv], axis=-1)
    theta = jnp.concatenate([theta, theta], axis=-1)
    sign = jnp.concatenate([-jnp.ones((HEAD_DIM // 2,), F32), jnp.ones((HEAD_DIM // 2,), F32)])
    cos = jnp.concatenate([jnp.cos(theta), jnp.ones((pad_rows, HEAD_DIM), F32)], axis=0)
    sin = jnp.concatenate([jnp.sin(theta) * sign, jnp.zeros((pad_rows, HEAD_DIM), F32)], axis=0)
    return cos, sin


def kernel(x, c, ctx, c_ctx, w_ada, b_ada, g_mix, g_mlp, w_in, attn_sink, conv_w, conv_b,
           conv_ln_g, conv_ln_b, w_out, w_mlp1, w_mlp2, g_final):
    n_batch, seq, d = x.shape
    ctx_len = ctx.shape[1]
    n_lat_rows = n_batch * seq
    assert d == D_MODEL and n_batch + 1 <= MOD_ROWS
    assert seq % TM_MLP == 0 and (n_batch * ctx_len) % TM_MLP == 0 and TQ % ctx_len == 0

    n_ctx_rows = n_batch * ctx_len
    rows = n_lat_rows + n_ctx_rows
    x2 = x.reshape(n_lat_rows, d)
    ctx2 = ctx.reshape(n_ctx_rows, d)
    cc = jnp.concatenate([c, c_ctx[None], jnp.zeros((MOD_ROWS - n_batch - 1, d), F32)], axis=0)
    mod = _ada_table(cc, w_ada, b_ada).reshape(DEPTH, MOD_ROWS, 1, 6 * d)
    cos_t, sin_t = _rope_tables(seq, TM_IN)

    w_in_b = w_in.astype(BF16)
    w_out_b = w_out.astype(BF16)
    w1_b = w_mlp1.astype(BF16)
    w2_b = w_mlp2.astype(BF16)
    g_mix3 = g_mix.reshape(DEPTH, 1, d)
    g_mlp3 = g_mlp.reshape(DEPTH, 1, d)
    gf = g_final.reshape(1, d)
    conv_b3 = conv_b.reshape(DEPTH, 1, CONV_CH)
    conv_lg3 = conv_ln_g.reshape(DEPTH, 1, CONV_CH)
    conv_lb3 = conv_ln_b.reshape(DEPTH, 1, CONV_CH)

    srcs = (x2, ctx2)
    for l in range(DEPTH):
        last = l == DEPTH - 1
        q, k, v, u, *joined = _inproj(srcs, mod, g_mix3, cos_t, sin_t, w_in_b, l, rows, n_lat_rows, seq)
        xa = joined[0] if joined else srcs[0]
        att = _attention(attn_sink, q, k, v, l, n_lat_rows, seq, ctx_len, with_ctx=not last)
        conv_args = (conv_w, conv_b3, conv_lg3, conv_lb3, l)
        xa = _outproj_conv(att, u, xa, mod, w_out_b, *conv_args,
                           n_lat_rows if last else rows, n_lat_rows, seq)
        if not last:
            cv_ctx = _conv_call(u, *conv_args, ctx_len, n_lat_rows, n_ctx_rows, 1)
            xa = _outproj(att, cv_ctx, xa, n_lat_rows, n_ctx_rows, mod, w_out_b, l, n_lat_rows, seq)
        xa = _mlp(xa, mod, g_mlp3, gf, w1_b, w2_b, l, n_lat_rows, seq, last)
        srcs = (xa,)
    return xa.reshape(n_batch, seq, d)
```

```python
import functools
import math

import jax
import jax.numpy as jnp
from jax import lax
from jax.experimental import pallas as pl
from jax.experimental.pallas import tpu as pltpu

F32 = jnp.float32
BF16 = jnp.bfloat16

D_MODEL = 2048
DEPTH = 4
GRID_W = 64
HEAD_DIM = 128
ATTN_HEADS = 8
KV_HEADS = 2
GQ = ATTN_HEADS // KV_HEADS
ATTN_W = ATTN_HEADS * HEAD_DIM
KV_W = KV_HEADS * HEAD_DIM
BLOCK = 128
ROPE_BASE = 10000.0
CONV_CH = D_MODEL // 2
CONV_K = 31
CONV_HALO = 16
MIX_W = ATTN_W + CONV_CH
IN_W = ATTN_W + 2 * KV_W + 2 * CONV_CH
D_FF = 4 * D_MODEL
EPS = 1e-6
NEG = -1e30

SUBLANES = 8
LANES = 128
V7X_VMEM_LIMIT = 56 * 1024 * 1024
MOD_ROWS = 8

TM_IN = 512
TQ = 512
TM_OUT = 512
TM_MLP = 1024
FC_MLP = 512
TN_ADA = 1024


def _params(n_axes, vmem=V7X_VMEM_LIMIT):
    return pltpu.CompilerParams(dimension_semantics=("arbitrary",) * n_axes,
                                vmem_limit_bytes=vmem)


def _layer_block(shape, layer):
    zeros = (0,) * len(shape)
    return pl.BlockSpec((None,) + tuple(shape), lambda *_: (layer,) + zeros,
                        pipeline_mode=pl.Buffered(1))


def _mod_spec(layer, tm, seq, n_batch, t0=0):
    return pl.BlockSpec((None, None, 1, 6 * D_MODEL),
                        lambda i, *_: (layer, jnp.minimum((t0 + i) * tm // seq, n_batch), 0, 0))


NORM_ROWS = 16
NORM_COLS = 256


def _rms_modulate_to(h_s, x_ref, g, shift, scale):
    gain = g * (1.0 + scale)
    for r0 in range(0, x_ref.shape[0], NORM_ROWS):
        rows = slice(r0, r0 + NORM_ROWS)
        x = x_ref[rows, :]
        rs = lax.rsqrt(jnp.mean(x * x, axis=-1, keepdims=True) + EPS)
        h_s[rows, :] = (x * rs * gain + shift).astype(BF16)


def _ada_kernel(c_ref, w_ref, b_ref, o_ref):
    c = c_ref[...]
    s = (c * jax.nn.sigmoid(c)).astype(BF16)
    o_ref[...] = jnp.dot(s, w_ref[...].astype(BF16), preferred_element_type=F32) + b_ref[...]


def _ada_table(cc, w_ada, b_ada):
    n_out = 6 * D_MODEL
    return pl.pallas_call(
        _ada_kernel,
        grid=(DEPTH, n_out // TN_ADA),
        in_specs=[
            pl.BlockSpec((MOD_ROWS, D_MODEL), lambda l, j: (0, 0)),
            pl.BlockSpec((None, D_MODEL, TN_ADA), lambda l, j: (l, 0, j)),
            pl.BlockSpec((None, 1, TN_ADA), lambda l, j: (l, 0, j)),
        ],
        out_specs=pl.BlockSpec((None, MOD_ROWS, TN_ADA), lambda l, j: (l, 0, j)),
        out_shape=jax.ShapeDtypeStruct((DEPTH, MOD_ROWS, n_out), F32),
        compiler_params=_params(2),
        name="ada_table",
    )(cc, w_ada, b_ada.reshape(DEPTH, 1, n_out))


def _inproj_kernel(*refs, n_lat_tiles, split_src):
    if split_src:
        xl_ref, xc_ref, mod_ref, g_ref, cos_ref, sin_ref, w_ref, q_ref, k_ref, v_ref, u_ref, xa_ref, h_s = refs
        xa_ref[...] = jnp.where(pl.program_id(0) < n_lat_tiles, xl_ref[...], xc_ref[...])
        x_ref = xa_ref
    else:
        x_ref, mod_ref, g_ref, cos_ref, sin_ref, w_ref, q_ref, k_ref, v_ref, u_ref, h_s = refs
    _rms_modulate_to(h_s, x_ref, g_ref[...], mod_ref[:, 0:D_MODEL], mod_ref[:, D_MODEL:2 * D_MODEL])
    cos = cos_ref[...]
    sin = sin_ref[...]

    def rope(seg):
        return seg * cos + pltpu.roll(seg, HEAD_DIM // 2, 1) * sin

    nq = 512
    for c in range(ATTN_W // nq):
        r = jnp.dot(h_s[...], w_ref[:, c * nq:(c + 1) * nq], preferred_element_type=F32)
        for hh in range(nq // HEAD_DIM):
            lo = hh * HEAD_DIM
            q_ref[:, c * nq + lo:c * nq + lo + HEAD_DIM] = rope(r[:, lo:lo + HEAD_DIM]).astype(BF16)
    r = jnp.dot(h_s[...], w_ref[:, ATTN_W:ATTN_W + 2 * KV_W], preferred_element_type=F32)
    for hh in range(KV_HEADS):
        lo = hh * HEAD_DIM
        k_ref[:, lo:lo + HEAD_DIM] = rope(r[:, lo:lo + HEAD_DIM]).astype(BF16)
    v_ref[...] = r[:, KV_W:2 * KV_W].astype(BF16)
    nu = 256
    u0 = ATTN_W + 2 * KV_W
    for c in range(CONV_CH // nu):
        a = jnp.dot(h_s[...], w_ref[:, u0 + c * nu:u0 + (c + 1) * nu], preferred_element_type=F32)
        gt = jnp.dot(h_s[...], w_ref[:, u0 + CONV_CH + c * nu:u0 + CONV_CH + (c + 1) * nu],
                     preferred_element_type=F32)
        u_ref[:, c * nu:(c + 1) * nu] = a * jax.nn.sigmoid(gt)


def _inproj(srcs, mod, g, cos_t, sin_t, w, layer, rows, n_lat_rows, seq):
    tm = TM_IN
    n_lat = n_lat_rows // tm
    per_seq = seq // tm
    n_batch = n_lat_rows // seq
    split_src = len(srcs) == 2

    def rope_idx(i):
        return (jnp.where(i < n_lat, i % per_seq, per_seq), 0)

    row = lambda i: (i, 0)
    if split_src:
        src_specs = [pl.BlockSpec((tm, D_MODEL), lambda i: (jnp.minimum(i, n_lat - 1), 0)),
                     pl.BlockSpec((tm, D_MODEL), lambda i: (jnp.maximum(i - n_lat, 0), 0))]
    else:
        src_specs = [pl.BlockSpec((tm, D_MODEL), row)]
    out_specs = [pl.BlockSpec((tm, ATTN_W), row), pl.BlockSpec((tm, KV_W), row),
                 pl.BlockSpec((tm, KV_W), row), pl.BlockSpec((tm, CONV_CH), row)]
    out_shape = [jax.ShapeDtypeStruct((rows, ATTN_W), BF16), jax.ShapeDtypeStruct((rows, KV_W), BF16),
                 jax.ShapeDtypeStruct((rows, KV_W), BF16), jax.ShapeDtypeStruct((rows, CONV_CH), F32)]
    if split_src:
        out_specs.append(pl.BlockSpec((tm, D_MODEL), row))
        out_shape.append(jax.ShapeDtypeStruct((rows, D_MODEL), F32))
    return pl.pallas_call(
        functools.partial(_inproj_kernel, n_lat_tiles=n_lat, split_src=split_src),
        grid=(rows // tm,),
        in_specs=src_specs + [
            _mod_spec(layer, tm, seq, n_batch),
            _layer_block((1, D_MODEL), layer),
            pl.BlockSpec((tm, HEAD_DIM), rope_idx),
            pl.BlockSpec((tm, HEAD_DIM), rope_idx),
            _layer_block((D_MODEL, IN_W), layer),
        ],
        out_specs=out_specs,
        out_shape=out_shape,
        scratch_shapes=[pltpu.VMEM((tm, D_MODEL), BF16)],
        compiler_params=_params(1),
        name="inproj",
    )(*srcs, mod, g, cos_t, sin_t, w)


SM_SCALE = HEAD_DIM ** -0.5
EXP2_SCALE = SM_SCALE * math.log2(math.e)


def _nt_dot(a, b):
    return lax.dot_general(a, b, (((1,), (1,)), ((), ())), preferred_element_type=F32)


def _stack_heads(q_ref, r0, nrows, kvh):
    return jnp.concatenate(
        [q_ref[r0:r0 + nrows, (kvh * GQ + g) * HEAD_DIM:(kvh * GQ + g + 1) * HEAD_DIM] for g in range(GQ)],
        axis=0)


def _sink_rows(sink_ref, layer, nrows, kvh):
    return jnp.concatenate(
        [jnp.full((nrows, LANES), sink_ref[layer, kvh * GQ + g] * (1.0 / SM_SCALE), F32)
         for g in range(GQ)], axis=0)


def _softmax_pv(chunks, sink, v_ext):
    mx = chunks[0]
    for ch in chunks[1:]:
        mx = jnp.maximum(mx, ch)
    m = jnp.maximum(jnp.max(mx, axis=-1, keepdims=True), sink)
    e = jnp.concatenate([jnp.exp2((ch - m) * EXP2_SCALE).astype(BF16) for ch in chunks], axis=1)
    o = jnp.dot(e, v_ext, preferred_element_type=F32)
    den = o[:, HEAD_DIM:] + jnp.exp2((sink - m) * EXP2_SCALE)
    return o[:, :HEAD_DIM] / den


def _attn_kernel(sink_ref, q_ref, km_ref, kp_ref, kn_ref, vm_ref, vp_ref, vn_ref, kc_ref, vc_ref,
                 o_ref, kbuf, vbuf, *, layer, n_lat_steps, steps_per_seq, ctx_len):
    i = pl.program_id(0)
    win = 3 * BLOCK
    nblk = TQ // BLOCK

    def store_heads(o, r0, nrows, kvh):
        for g in range(GQ):
            c0 = (kvh * GQ + g) * HEAD_DIM
            o_ref[r0:r0 + nrows, c0:c0 + HEAD_DIM] = o[g * nrows:(g + 1) * nrows].astype(BF16)

    @pl.when(i < n_lat_steps)
    def _latent():
        kbuf[0:BLOCK] = kp_ref[...]
        kbuf[BLOCK:BLOCK + TQ] = km_ref[...]
        kbuf[BLOCK + TQ:2 * BLOCK + TQ] = kn_ref[...]
        vbuf[0:BLOCK] = vp_ref[...]
        vbuf[BLOCK:BLOCK + TQ] = vm_ref[...]
        vbuf[BLOCK + TQ:2 * BLOCK + TQ] = vn_ref[...]
        pos = i % steps_per_seq
        off_first = jnp.where(pos == 0, BLOCK, 0)
        off_last = jnp.where(pos == steps_per_seq - 1, BLOCK, 0)
        r = lax.broadcasted_iota(jnp.int32, (GQ * BLOCK, LANES), 0) & (BLOCK - 1)
        j = lax.broadcasted_iota(jnp.int32, (GQ * BLOCK, LANES), 1)
        ones = jnp.ones((win + ctx_len, HEAD_DIM), BF16)

        def scores(t):
            blk, kvh = divmod(t, KV_HEADS)
            r0, c0 = blk * BLOCK, kvh * HEAD_DIM
            qg = _stack_heads(q_ref, r0, BLOCK, kvh)
            keys = jnp.concatenate([kbuf[r0:r0 + win, c0:c0 + HEAD_DIM], kc_ref[:, c0:c0 + HEAD_DIM]],
                                   axis=0)
            return _nt_dot(qg, keys)

        def finish(t, s):
            blk, kvh = divmod(t, KV_HEADS)
            r0, c0 = blk * BLOCK, kvh * HEAD_DIM
            lo = r + off_first if blk == 0 else r
            hi = r - off_last if blk == nblk - 1 else r
            chunks = [s[:, n * LANES:(n + 1) * LANES] for n in range((win + ctx_len) // LANES)]
            chunks[0] = jnp.where(j >= lo, chunks[0], NEG)
            chunks[2] = jnp.where(j <= hi, chunks[2], NEG)
            vals = jnp.concatenate([vbuf[r0:r0 + win, c0:c0 + HEAD_DIM], vc_ref[:, c0:c0 + HEAD_DIM]],
                                   axis=0)
            v_ext = jnp.concatenate([vals, ones], axis=1)
            o = _softmax_pv(chunks, _sink_rows(sink_ref, layer, BLOCK, kvh), v_ext)
            store_heads(o, r0, BLOCK, kvh)

        n_items = nblk * KV_HEADS
        s_next = scores(0)
        for t in range(n_items):
            s_cur = s_next
            if t + 1 < n_items:
                s_next = scores(t + 1)
            finish(t, s_cur)

    @pl.when(i >= n_lat_steps)
    def _context():
        ones = jnp.ones((ctx_len, HEAD_DIM), BF16)
        for bb in range(TQ // ctx_len):
            r0 = bb * ctx_len
            for kvh in range(KV_HEADS):
                c0 = kvh * HEAD_DIM
                qg = _stack_heads(q_ref, r0, ctx_len, kvh)
                s = _nt_dot(qg, km_ref[r0:r0 + ctx_len, c0:c0 + HEAD_DIM])
                chunks = [s[:, n * LANES:(n + 1) * LANES] for n in range(ctx_len // LANES)]
                v_ext = jnp.concatenate([vm_ref[r0:r0 + ctx_len, c0:c0 + HEAD_DIM], ones], axis=1)
                o = _softmax_pv(chunks, _sink_rows(sink_ref, layer, ctx_len, kvh), v_ext)
                store_heads(o, r0, ctx_len, kvh)


def _attention(sink, q, k, v, layer, n_lat_rows, seq, ctx_len, with_ctx):
    rows = q.shape[0]
    n_lat_steps = n_lat_rows // TQ
    steps = rows // TQ if with_ctx else n_lat_steps
    steps_per_seq = seq // TQ
    n_batch = n_lat_rows // seq
    hb = TQ // BLOCK
    last_blk = rows // BLOCK - 1

    main = lambda i: (i, 0)
    prev = lambda i: (jnp.maximum(i * hb - 1, 0), 0)
    nxt = lambda i: (jnp.minimum((i + 1) * hb, last_blk), 0)
    ctx = lambda i: (n_lat_rows // ctx_len + jnp.minimum(i // steps_per_seq, n_batch - 1), 0)
    kv_specs = [pl.BlockSpec((TQ, KV_W), main), pl.BlockSpec((BLOCK, KV_W), prev),
                pl.BlockSpec((BLOCK, KV_W), nxt)]
    return pl.pallas_call(
        functools.partial(_attn_kernel, layer=layer, n_lat_steps=n_lat_steps,
                          steps_per_seq=steps_per_seq, ctx_len=ctx_len),
        grid=(steps,),
        in_specs=[pl.BlockSpec(memory_space=pltpu.SMEM), pl.BlockSpec((TQ, ATTN_W), main)]
                 + kv_specs + kv_specs
                 + [pl.BlockSpec((ctx_len, KV_W), ctx), pl.BlockSpec((ctx_len, KV_W), ctx)],
        out_specs=pl.BlockSpec((TQ, ATTN_W), main),
        out_shape=jax.ShapeDtypeStruct((steps * TQ, ATTN_W), BF16),
        scratch_shapes=[pltpu.VMEM((TQ + 2 * BLOCK, KV_W), BF16),
                        pltpu.VMEM((TQ + 2 * BLOCK, KV_W), BF16)],
        compiler_params=_params(1),
        name="attention",
    )(sink, q, k, k, k, v, v, v, k, v)


CONV_SHIFT_PAD = 24


CONV_HALF = CONV_CH // 2
CONV_ROWS = 128


def _conv_tile(u_ref, up_ref, un_ref, has_prev, has_next, w_ref, b_ref, lg_ref, lb_ref,
               ext, ysh, cbuf, tc, store, side_work=()):
    n_sh = tc + CONV_SHIFT_PAD
    rc = min(tc, CONV_ROWS)
    first = CONV_HALO - CONV_K // 2
    side = list(side_work)

    def fill_ext():
        if up_ref is None:
            ext[0:CONV_HALO] = jnp.zeros((CONV_HALO, CONV_CH), F32)
        else:
            ext[0:CONV_HALO] = jnp.where(has_prev, up_ref[...], 0.0)
        ext[CONV_HALO:CONV_HALO + tc] = u_ref[...]
        ext[CONV_HALO + tc:2 * CONV_HALO + tc] = jnp.where(has_next, un_ref[...], 0.0)

    def shift_copies(h0):
        for s in range(1, SUBLANES):
            ysh[s - 1] = ext[s:s + n_sh, h0:h0 + CONV_HALF]

    def taps(h0, cc, work):
        l0 = cc * LANES
        c0 = h0 + l0
        if work is not None:
            work()
        for rb in range(tc // rc):
            acc = jnp.zeros((rc, LANES), F32)
            for tap in range(CONV_K):
                q8, s = divmod(first + tap, SUBLANES)
                r0 = SUBLANES * q8 + rb * rc
                if s == 0:
                    rows = ext[r0:r0 + rc, c0:c0 + LANES]
                else:
                    rows = ysh[s - 1, r0:r0 + rc, l0:l0 + LANES]
                acc = acc + rows * w_ref[tap:tap + 1, c0:c0 + LANES]
            cbuf[rb * rc:(rb + 1) * rc, c0:c0 + LANES] = acc + b_ref[:, c0:c0 + LANES]

    def norm_act():
        chunks = [slice(c0, c0 + NORM_COLS) for c0 in range(0, CONV_CH, NORM_COLS)]
        mu = jnp.mean(cbuf[...], axis=-1, keepdims=True)
        sq = [jnp.sum(jnp.square(cbuf[:, cols] - mu), axis=-1, keepdims=True) for cols in chunks]
        rs = lax.rsqrt(functools.reduce(lambda a, b: a + b, sq) * (1.0 / CONV_CH) + EPS)
        for cols in chunks:
            y = (cbuf[:, cols] - mu) * rs * lg_ref[:, cols] + lb_ref[:, cols]
            store(cols, (y * jax.nn.sigmoid(y)).astype(BF16))

    fill_ext()
    for half in range(CONV_CH // CONV_HALF):
        h0 = half * CONV_HALF
        shift_copies(h0)
        for cc in range(CONV_HALF // LANES):
            taps(h0, cc, side.pop(0) if side else None)
    for work in side:
        work()
    norm_act()


def _conv_scratch(tc):
    return [pltpu.VMEM((tc + 2 * CONV_HALO, CONV_CH), F32),
            pltpu.VMEM((SUBLANES - 1, tc + CONV_SHIFT_PAD, CONV_HALF), F32),
            pltpu.VMEM((tc, CONV_CH), F32)]


def _conv_kernel(u_ref, up_ref, un_ref, w_ref, b_ref, lg_ref, lb_ref, o_ref,
                 ext, ysh, cbuf, *, tc, tiles_per_seq):
    pos = pl.program_id(0) % tiles_per_seq

    def store(cols, val):
        o_ref[:, cols] = val

    _conv_tile(u_ref, up_ref, un_ref, pos != 0, pos != tiles_per_seq - 1,
               w_ref, b_ref, lg_ref, lb_ref, ext, ysh, cbuf, tc, store)


def _conv_call(u, w, b, lg, lb, layer, tc, row0, n_rows, tiles_per_seq):
    rows = u.shape[0]
    t0 = row0 // tc
    hb = tc // CONV_HALO
    last_h = rows // CONV_HALO - 1
    return pl.pallas_call(
        functools.partial(_conv_kernel, tc=tc, tiles_per_seq=tiles_per_seq),
        grid=(n_rows // tc,),
        in_specs=[
            pl.BlockSpec((tc, CONV_CH), lambda i: (t0 + i, 0)),
            pl.BlockSpec((CONV_HALO, CONV_CH), lambda i: (jnp.maximum((t0 + i) * hb - 1, 0), 0)),
            pl.BlockSpec((CONV_HALO, CONV_CH), lambda i: (jnp.minimum((t0 + i + 1) * hb, last_h), 0)),
            _layer_block((CONV_K, CONV_CH), layer),
            _layer_block((1, CONV_CH), layer),
            _layer_block((1, CONV_CH), layer),
            _layer_block((1, CONV_CH), layer),
        ],
        out_specs=pl.BlockSpec((tc, CONV_CH), lambda i: (i, 0)),
        out_shape=jax.ShapeDtypeStruct((n_rows, CONV_CH), BF16),
        scratch_shapes=_conv_scratch(tc),
        compiler_params=_params(1),
        name="conv",
    )(u, u, u, w, b, lg, lb)


OUT_CHUNK = 256


def _outproj_chunk(c, att_ref, load_cv, x_ref, mod_ref, w_ref, o_ref):
    cols = slice(c * OUT_CHUNK, (c + 1) * OUT_CHUNK)
    mix = jnp.concatenate([att_ref[...], load_cv()], axis=1)
    r = jnp.dot(mix, w_ref[:, cols], preferred_element_type=F32)
    g1 = mod_ref[:, 2 * D_MODEL + c * OUT_CHUNK:2 * D_MODEL + (c + 1) * OUT_CHUNK]
    o_ref[:, cols] = x_ref[:, cols] + g1 * r


def _outproj_kernel(att_ref, cv_ref, x_ref, mod_ref, w_ref, o_ref):
    for c in range(D_MODEL // OUT_CHUNK):
        _outproj_chunk(c, att_ref, lambda: cv_ref[...], x_ref, mod_ref, w_ref, o_ref)


def _outproj(att, cv, xa, row0, n_rows, mod, w, layer, n_lat_rows, seq):
    tm = TM_OUT
    n_batch = n_lat_rows // seq
    t0 = row0 // tm
    row = lambda i: (t0 + i, 0)
    return pl.pallas_call(
        _outproj_kernel,
        grid=(n_rows // tm,),
        in_specs=[
            pl.BlockSpec((tm, ATTN_W), row),
            pl.BlockSpec((tm, CONV_CH), lambda i: (i, 0)),
            pl.BlockSpec((tm, D_MODEL), row),
            _mod_spec(layer, tm, seq, n_batch, t0),
            _layer_block((MIX_W, D_MODEL), layer),
        ],
        out_specs=pl.BlockSpec((tm, D_MODEL), row),
        out_shape=jax.ShapeDtypeStruct(xa.shape, F32),
        input_output_aliases={2: 0},
        compiler_params=_params(1),
        name="outproj",
    )(att, cv, xa, mod, w)


def _outproj_conv_kernel(att_ref, x_ref, mod_ref, w_ref, u0_ref, u0n_ref, un_ref, unp_ref, unn_ref,
                         cw_ref, cb_ref, lg_ref, lb_ref, o_ref, cvbuf, ext, ysh, cbuf,
                         *, tc, tiles_per_seq):
    i = pl.program_id(0)
    conv_refs = (cw_ref, cb_ref, lg_ref, lb_ref, ext, ysh, cbuf, tc)

    def store_first(cols, val):
        cvbuf[0, :, cols] = val

    @pl.when(i == 0)
    def _first():
        _conv_tile(u0_ref, None, u0n_ref, False, tiles_per_seq > 1, *conv_refs, store_first)

    nxt = jnp.minimum(i + 1, pl.num_programs(0) - 1)
    pos = nxt % tiles_per_seq
    slot = i % 2

    def store_next(cols, val):
        cvbuf[1 - slot, :, cols] = val

    chunks = [functools.partial(_outproj_chunk, c, att_ref, lambda: cvbuf[slot], x_ref, mod_ref, w_ref, o_ref)
              for c in range(D_MODEL // OUT_CHUNK)]
    _conv_tile(un_ref, unp_ref, unn_ref, pos != 0, pos != tiles_per_seq - 1, *conv_refs, store_next,
               side_work=chunks)


def _outproj_conv(att, u, res, mod, w, cw, cb, lg, lb, layer, out_rows, n_lat_rows, seq):
    tm = TM_OUT
    n_tiles = n_lat_rows // tm
    tiles_per_seq = seq // tm
    n_batch = n_lat_rows // seq
    hb = tm // CONV_HALO
    row = lambda i: (i, 0)
    nxt = lambda i: jnp.minimum(i + 1, n_tiles - 1)
    const = dict(pipeline_mode=pl.Buffered(1))
    return pl.pallas_call(
        functools.partial(_outproj_conv_kernel, tc=tm, tiles_per_seq=tiles_per_seq),
        grid=(n_tiles,),
        in_specs=[
            pl.BlockSpec((tm, ATTN_W), row),
            pl.BlockSpec((tm, D_MODEL), row),
            _mod_spec(layer, tm, seq, n_batch),
            _layer_block((MIX_W, D_MODEL), layer),
            pl.BlockSpec((tm, CONV_CH), lambda i: (0, 0), **const),
            pl.BlockSpec((CONV_HALO, CONV_CH), lambda i: (hb, 0), **const),
            pl.BlockSpec((tm, CONV_CH), lambda i: (nxt(i), 0)),
            pl.BlockSpec((CONV_HALO, CONV_CH), lambda i: (nxt(i) * hb - 1, 0)),
            pl.BlockSpec((CONV_HALO, CONV_CH), lambda i: ((nxt(i) + 1) * hb, 0)),
            _layer_block((CONV_K, CONV_CH), layer),
            _layer_block((1, CONV_CH), layer),
            _layer_block((1, CONV_CH), layer),
            _layer_block((1, CONV_CH), layer),
        ],
        out_specs=pl.BlockSpec((tm, D_MODEL), row),
        out_shape=jax.ShapeDtypeStruct((out_rows, D_MODEL), F32),
        scratch_shapes=[pltpu.VMEM((2, tm, CONV_CH), BF16)] + _conv_scratch(tm),
        input_output_aliases={1: 0} if res.shape[0] == out_rows else {},
        compiler_params=_params(1),
        name="outproj_conv",
    )(att, res, mod, w, u, u, u, u, u, cw, cb, lg, lb)


def _mlp_kernel(x_ref, mod_ref, g_ref, gf_ref, w1_ref, w2_ref, o_ref, h_s, *, final):
    j = pl.program_id(1)

    @pl.when(j == 0)
    def _prologue():
        _rms_modulate_to(h_s, x_ref, g_ref[...],
                         mod_ref[:, 3 * D_MODEL:4 * D_MODEL], mod_ref[:, 4 * D_MODEL:5 * D_MODEL])
        o_ref[...] = jnp.zeros_like(o_ref)

    a = jnp.dot(h_s[...], w1_ref[...], preferred_element_type=F32)
    a = jnp.maximum(a, 0.0)
    o_ref[...] += jnp.dot((a * a).astype(BF16), w2_ref[...], preferred_element_type=F32)

    @pl.when(j == pl.num_programs(1) - 1)
    def _epilogue():
        g2 = mod_ref[:, 5 * D_MODEL:6 * D_MODEL]
        if not final:
            o_ref[...] = x_ref[...] + g2 * o_ref[...]
        else:
            for r0 in range(0, o_ref.shape[0], NORM_ROWS):
                rows = slice(r0, r0 + NORM_ROWS)
                y = x_ref[rows, :] + g2 * o_ref[rows, :]
                rs = lax.rsqrt(jnp.mean(y * y, axis=-1, keepdims=True) + EPS)
                o_ref[rows, :] = y * rs * gf_ref[...]


def _mlp(xa, mod, g, gf, w1, w2, layer, n_lat_rows, seq, final):
    rows = xa.shape[0]
    tm, fc = TM_MLP, FC_MLP
    n_batch = n_lat_rows // seq
    row = lambda i, j: (i, 0)
    return pl.pallas_call(
        functools.partial(_mlp_kernel, final=final),
        grid=(rows // tm, D_FF // fc),
        in_specs=[
            pl.BlockSpec((tm, D_MODEL), row),
            _mod_spec(layer, tm, seq, n_batch),
            _layer_block((1, D_MODEL), layer),
            pl.BlockSpec((1, D_MODEL), lambda i, j: (0, 0)),
            pl.BlockSpec((None, D_MODEL, fc), lambda i, j: (layer, 0, j)),
            pl.BlockSpec((None, fc, D_MODEL), lambda i, j: (layer, j, 0)),
        ],
        out_specs=pl.BlockSpec((tm, D_MODEL), row),
        out_shape=jax.ShapeDtypeStruct((rows, D_MODEL), F32),
        scratch_shapes=[pltpu.VMEM((tm, D_MODEL), BF16)],
        input_output_aliases={0: 0},
        compiler_params=_params(2),
        name="mlp",
    )(xa, mod, g, gf, w1, w2)


def _rope_tables(seq, pad_rows):
    t = jnp.arange(seq, dtype=jnp.int32)
    row = (t // GRID_W).astype(F32)
    col = (t % GRID_W).astype(F32)
    n_freq = HEAD_DIM // 4
    inv = ROPE_BASE ** (-jnp.arange(n_freq, dtype=F32) / n_freq)
    theta = jnp.concatenate([row[:, None] * inv, col[:, None] * inv], axis=-1)
    theta = jnp.concatenate([theta, theta], axis=-1)
    sign = jnp.concatenate([-jnp.ones((HEAD_DIM // 2,), F32), jnp.ones((HEAD_DIM // 2,), F32)])
    cos = jnp.concatenate([jnp.cos(theta), jnp.ones((pad_rows, HEAD_DIM), F32)], axis=0)
    sin = jnp.concatenate([jnp.sin(theta) * sign, jnp.zeros((pad_rows, HEAD_DIM), F32)], axis=0)
    return cos, sin


def kernel(x, c, ctx, c_ctx, w_ada, b_ada, g_mix, g_mlp, w_in, attn_sink, conv_w, conv_b,
           conv_ln_g, conv_ln_b, w_out, w_mlp1, w_mlp2, g_final):
    n_batch, seq, d = x.shape
    ctx_len = ctx.shape[1]
    n_lat_rows = n_batch * seq
    assert d == D_MODEL and n_batch + 1 <= MOD_ROWS
    assert seq % TM_MLP == 0 and (n_batch * ctx_len) % TM_MLP == 0 and TQ % ctx_len == 0

    n_ctx_rows = n_batch * ctx_len
    rows = n_lat_rows + n_ctx_rows
    x2 = x.reshape(n_lat_rows, d)
    ctx2 = ctx.reshape(n_ctx_rows, d)
    cc = jnp.concatenate([c, c_ctx[None], jnp.zeros((MOD_ROWS - n_batch - 1, d), F32)], axis=0)
    mod = _ada_table(cc, w_ada, b_ada).reshape(DEPTH, MOD_ROWS, 1, 6 * d)
    cos_t, sin_t = _rope_tables(seq, TM_IN)

    w_in_b = w_in.astype(BF16)
    w_out_b = w_out.astype(BF16)
    w1_b = w_mlp1.astype(BF16)
    w2_b = w_mlp2.astype(BF16)
    g_mix3 = g_mix.reshape(DEPTH, 1, d)
    g_mlp3 = g_mlp.reshape(DEPTH, 1, d)
    gf = g_final.reshape(1, d)
    conv_b3 = conv_b.reshape(DEPTH, 1, CONV_CH)
    conv_lg3 = conv_ln_g.reshape(DEPTH, 1, CONV_CH)
    conv_lb3 = conv_ln_b.reshape(DEPTH, 1, CONV_CH)

    srcs = (x2, ctx2)
    for l in range(DEPTH):
        last = l == DEPTH - 1
        q, k, v, u, *joined = _inproj(srcs, mod, g_mix3, cos_t, sin_t, w_in_b, l, rows, n_lat_rows, seq)
        xa = joined[0] if joined else srcs[0]
        att = _attention(attn_sink, q, k, v, l, n_lat_rows, seq, ctx_len, with_ctx=not last)
        conv_args = (conv_w, conv_b3, conv_lg3, conv_lb3, l)
        xa = _outproj_conv(att, u, xa, mod, w_out_b, *conv_args,
                           n_lat_rows if last else rows, n_lat_rows, seq)
        if not last:
            cv_ctx = _conv_call(u, *conv_args, ctx_len, n_lat_rows, n_ctx_rows, 1)
            xa = _outproj(att, cv_ctx, xa, n_lat_rows, n_ctx_rows, mod, w_out_b, l, n_lat_rows, seq)
        xa = _mlp(xa, mod, g_mlp3, gf, w1_b, w2_b, l, n_lat_rows, seq, last)
        srcs = (xa,)
    return xa.reshape(n_batch, seq, d)
```

```python
import functools
import math

import jax
import jax.numpy as jnp
from jax import lax
from jax.experimental import pallas as pl
from jax.experimental.pallas import tpu as pltpu

F32 = jnp.float32
BF16 = jnp.bfloat16

D_MODEL = 2048
DEPTH = 4
GRID_W = 64
HEAD_DIM = 128
ATTN_HEADS = 8
KV_HEADS = 2
GQ = ATTN_HEADS // KV_HEADS
ATTN_W = ATTN_HEADS * HEAD_DIM
KV_W = KV_HEADS * HEAD_DIM
BLOCK = 128
ROPE_BASE = 10000.0
CONV_CH = D_MODEL // 2
CONV_K = 31
CONV_HALO = 16
MIX_W = ATTN_W + CONV_CH
IN_W = ATTN_W + 2 * KV_W + 2 * CONV_CH
D_FF = 4 * D_MODEL
EPS = 1e-6
NEG = -1e30

SUBLANES = 8
LANES = 128
V7X_VMEM_LIMIT = 56 * 1024 * 1024
MOD_ROWS = 8

TM_IN = 512
TQ = 512
TM_OUT = 512
TM_MLP = 1024
FC_MLP = 512
TN_ADA = 1024


def _params(n_axes, vmem=V7X_VMEM_LIMIT):
    return pltpu.CompilerParams(dimension_semantics=("arbitrary",) * n_axes,
                                vmem_limit_bytes=vmem)


SINGLE_BUFFER_BYTES = 1 << 20


def _layer_block(shape, layer, itemsize):
    zeros = (0,) * len(shape)
    big = math.prod(shape) * itemsize >= SINGLE_BUFFER_BYTES
    mode = dict(pipeline_mode=pl.Buffered(1)) if big else {}
    return pl.BlockSpec((None,) + tuple(shape), lambda *_: (layer,) + zeros, **mode)


def _mod_spec(layer, tm, seq, n_batch, t0=0):
    return pl.BlockSpec((None, None, 1, 6 * D_MODEL),
                        lambda i, *_: (layer, jnp.minimum((t0 + i) * tm // seq, n_batch), 0, 0))


NORM_ROWS = 16
NORM_COLS = 256


def _rms_modulate_to(h_s, x_ref, g, shift, scale):
    gain = g * (1.0 + scale)
    for r0 in range(0, x_ref.shape[0], NORM_ROWS):
        rows = slice(r0, r0 + NORM_ROWS)
        x = x_ref[rows, :]
        rs = lax.rsqrt(jnp.mean(x * x, axis=-1, keepdims=True) + EPS)
        h_s[rows, :] = (x * rs * gain + shift).astype(BF16)


def _ada_kernel(c_ref, w_ref, b_ref, o_ref):
    c = c_ref[...]
    s = (c * jax.nn.sigmoid(c)).astype(BF16)
    o_ref[...] = jnp.dot(s, w_ref[...].astype(BF16), preferred_element_type=F32) + b_ref[...]


def _ada_table(cc, w_ada, b_ada):
    n_out = 6 * D_MODEL
    return pl.pallas_call(
        _ada_kernel,
        grid=(DEPTH, n_out // TN_ADA),
        in_specs=[
            pl.BlockSpec((MOD_ROWS, D_MODEL), lambda l, j: (0, 0)),
            pl.BlockSpec((None, D_MODEL, TN_ADA), lambda l, j: (l, 0, j)),
            pl.BlockSpec((None, 1, TN_ADA), lambda l, j: (l, 0, j)),
        ],
        out_specs=pl.BlockSpec((None, MOD_ROWS, TN_ADA), lambda l, j: (l, 0, j)),
        out_shape=jax.ShapeDtypeStruct((DEPTH, MOD_ROWS, n_out), F32),
        compiler_params=_params(2),
        name="ada_table",
    )(cc, w_ada, b_ada.reshape(DEPTH, 1, n_out))


def _inproj_kernel(*refs, n_lat_tiles, split_src):
    if split_src:
        xl_ref, xc_ref, mod_ref, g_ref, cos_ref, sin_ref, w_ref, q_ref, k_ref, v_ref, u_ref, xa_ref, h_s = refs
        xa_ref[...] = jnp.where(pl.program_id(0) < n_lat_tiles, xl_ref[...], xc_ref[...])
        x_ref = xa_ref
    else:
        x_ref, mod_ref, g_ref, cos_ref, sin_ref, w_ref, q_ref, k_ref, v_ref, u_ref, h_s = refs
    _rms_modulate_to(h_s, x_ref, g_ref[...], mod_ref[:, 0:D_MODEL], mod_ref[:, D_MODEL:2 * D_MODEL])
    cos = cos_ref[...]
    sin = sin_ref[...]

    def rope(seg):
        return seg * cos + pltpu.roll(seg, HEAD_DIM // 2, 1) * sin

    nq = 512
    for c in range(ATTN_W // nq):
        r = jnp.dot(h_s[...], w_ref[:, c * nq:(c + 1) * nq], preferred_element_type=F32)
        for hh in range(nq // HEAD_DIM):
            lo = hh * HEAD_DIM
            q_ref[:, c * nq + lo:c * nq + lo + HEAD_DIM] = rope(r[:, lo:lo + HEAD_DIM]).astype(BF16)
    r = jnp.dot(h_s[...], w_ref[:, ATTN_W:ATTN_W + 2 * KV_W], preferred_element_type=F32)
    for hh in range(KV_HEADS):
        lo = hh * HEAD_DIM
        k_ref[:, lo:lo + HEAD_DIM] = rope(r[:, lo:lo + HEAD_DIM]).astype(BF16)
    v_ref[...] = r[:, KV_W:2 * KV_W].astype(BF16)
    nu = 256
    u0 = ATTN_W + 2 * KV_W
    for c in range(CONV_CH // nu):
        a = jnp.dot(h_s[...], w_ref[:, u0 + c * nu:u0 + (c + 1) * nu], preferred_element_type=F32)
        gt = jnp.dot(h_s[...], w_ref[:, u0 + CONV_CH + c * nu:u0 + CONV_CH + (c + 1) * nu],
                     preferred_element_type=F32)
        u_ref[:, c * nu:(c + 1) * nu] = a * jax.nn.sigmoid(gt)


def _inproj(srcs, mod, g, cos_t, sin_t, w, layer, rows, n_lat_rows, seq):
    tm = TM_IN
    n_lat = n_lat_rows // tm
    per_seq = seq // tm
    n_batch = n_lat_rows // seq
    split_src = len(srcs) == 2

    def rope_idx(i):
        return (jnp.where(i < n_lat, i % per_seq, per_seq), 0)

    row = lambda i: (i, 0)
    if split_src:
        src_specs = [pl.BlockSpec((tm, D_MODEL), lambda i: (jnp.minimum(i, n_lat - 1), 0)),
                     pl.BlockSpec((tm, D_MODEL), lambda i: (jnp.maximum(i - n_lat, 0), 0))]
    else:
        src_specs = [pl.BlockSpec((tm, D_MODEL), row)]
    out_specs = [pl.BlockSpec((tm, ATTN_W), row), pl.BlockSpec((tm, KV_W), row),
                 pl.BlockSpec((tm, KV_W), row), pl.BlockSpec((tm, CONV_CH), row)]
    out_shape = [jax.ShapeDtypeStruct((rows, ATTN_W), BF16), jax.ShapeDtypeStruct((rows, KV_W), BF16),
                 jax.ShapeDtypeStruct((rows, KV_W), BF16), jax.ShapeDtypeStruct((rows, CONV_CH), F32)]
    if split_src:
        out_specs.append(pl.BlockSpec((tm, D_MODEL), row))
        out_shape.append(jax.ShapeDtypeStruct((rows, D_MODEL), F32))
    return pl.pallas_call(
        functools.partial(_inproj_kernel, n_lat_tiles=n_lat, split_src=split_src),
        grid=(rows // tm,),
        in_specs=src_specs + [
            _mod_spec(layer, tm, seq, n_batch),
            _layer_block((1, D_MODEL), layer, 4),
            pl.BlockSpec((tm, HEAD_DIM), rope_idx),
            pl.BlockSpec((tm, HEAD_DIM), rope_idx),
            _layer_block((D_MODEL, IN_W), layer, 2),
        ],
        out_specs=out_specs,
        out_shape=out_shape,
        scratch_shapes=[pltpu.VMEM((tm, D_MODEL), BF16)],
        compiler_params=_params(1),
        name="inproj",
    )(*srcs, mod, g, cos_t, sin_t, w)


SM_SCALE = HEAD_DIM ** -0.5
EXP2_SCALE = SM_SCALE * math.log2(math.e)


def _nt_dot(a, b):
    return lax.dot_general(a, b, (((1,), (1,)), ((), ())), preferred_element_type=F32)


def _stack_heads(q_ref, r0, nrows, kvh):
    return jnp.concatenate(
        [q_ref[r0:r0 + nrows, (kvh * GQ + g) * HEAD_DIM:(kvh * GQ + g + 1) * HEAD_DIM] for g in range(GQ)],
        axis=0)


def _sink_rows(sink_ref, layer, nrows, kvh):
    return jnp.concatenate(
        [jnp.full((nrows, LANES), sink_ref[layer, kvh * GQ + g] * (1.0 / SM_SCALE), F32)
         for g in range(GQ)], axis=0)


def _softmax_pv(chunks, sink, v_ext):
    mx = chunks[0]
    for ch in chunks[1:]:
        mx = jnp.maximum(mx, ch)
    m = jnp.maximum(jnp.max(mx, axis=-1, keepdims=True), sink)
    e = jnp.concatenate([jnp.exp2((ch - m) * EXP2_SCALE).astype(BF16) for ch in chunks], axis=1)
    o = jnp.dot(e, v_ext, preferred_element_type=F32)
    den = o[:, HEAD_DIM:] + jnp.exp2((sink - m) * EXP2_SCALE)
    return o[:, :HEAD_DIM] / den


def _attn_kernel(sink_ref, q_ref, km_ref, kp_ref, kn_ref, vm_ref, vp_ref, vn_ref, kc_ref, vc_ref,
                 o_ref, kbuf, vbuf, *, layer, n_lat_steps, steps_per_seq, ctx_len):
    i = pl.program_id(0)
    win = 3 * BLOCK
    nblk = TQ // BLOCK

    def store_heads(o, r0, nrows, kvh):
        for g in range(GQ):
            c0 = (kvh * GQ + g) * HEAD_DIM
            o_ref[r0:r0 + nrows, c0:c0 + HEAD_DIM] = o[g * nrows:(g + 1) * nrows].astype(BF16)

    @pl.when(i < n_lat_steps)
    def _latent():
        kbuf[0:BLOCK] = kp_ref[...]
        kbuf[BLOCK:BLOCK + TQ] = km_ref[...]
        kbuf[BLOCK + TQ:2 * BLOCK + TQ] = kn_ref[...]
        vbuf[0:BLOCK] = vp_ref[...]
        vbuf[BLOCK:BLOCK + TQ] = vm_ref[...]
        vbuf[BLOCK + TQ:2 * BLOCK + TQ] = vn_ref[...]
        pos = i % steps_per_seq
        off_first = jnp.where(pos == 0, BLOCK, 0)
        off_last = jnp.where(pos == steps_per_seq - 1, BLOCK, 0)
        r = lax.broadcasted_iota(jnp.int32, (GQ * BLOCK, LANES), 0) & (BLOCK - 1)
        j = lax.broadcasted_iota(jnp.int32, (GQ * BLOCK, LANES), 1)
        ones = jnp.ones((win + ctx_len, HEAD_DIM), BF16)

        def scores(t):
            blk, kvh = divmod(t, KV_HEADS)
            r0, c0 = blk * BLOCK, kvh * HEAD_DIM
            qg = _stack_heads(q_ref, r0, BLOCK, kvh)
            keys = jnp.concatenate([kbuf[r0:r0 + win, c0:c0 + HEAD_DIM], kc_ref[:, c0:c0 + HEAD_DIM]],
                                   axis=0)
            return _nt_dot(qg, keys)

        def finish(t, s):
            blk, kvh = divmod(t, KV_HEADS)
            r0, c0 = blk * BLOCK, kvh * HEAD_DIM
            lo = r + off_first if blk == 0 else r
            hi = r - off_last if blk == nblk - 1 else r
            chunks = [s[:, n * LANES:(n + 1) * LANES] for n in range((win + ctx_len) // LANES)]
            chunks[0] = jnp.where(j >= lo, chunks[0], NEG)
            chunks[2] = jnp.where(j <= hi, chunks[2], NEG)
            vals = jnp.concatenate([vbuf[r0:r0 + win, c0:c0 + HEAD_DIM], vc_ref[:, c0:c0 + HEAD_DIM]],
                                   axis=0)
            v_ext = jnp.concatenate([vals, ones], axis=1)
            o = _softmax_pv(chunks, _sink_rows(sink_ref, layer, BLOCK, kvh), v_ext)
            store_heads(o, r0, BLOCK, kvh)

        n_items = nblk * KV_HEADS
        s_next = scores(0)
        for t in range(n_items):
            s_cur = s_next
            if t + 1 < n_items:
                s_next = scores(t + 1)
            finish(t, s_cur)

    @pl.when(i >= n_lat_steps)
    def _context():
        ones = jnp.ones((ctx_len, HEAD_DIM), BF16)
        for bb in range(TQ // ctx_len):
            r0 = bb * ctx_len
            for kvh in range(KV_HEADS):
                c0 = kvh * HEAD_DIM
                qg = _stack_heads(q_ref, r0, ctx_len, kvh)
                s = _nt_dot(qg, km_ref[r0:r0 + ctx_len, c0:c0 + HEAD_DIM])
                chunks = [s[:, n * LANES:(n + 1) * LANES] for n in range(ctx_len // LANES)]
                v_ext = jnp.concatenate([vm_ref[r0:r0 + ctx_len, c0:c0 + HEAD_DIM], ones], axis=1)
                o = _softmax_pv(chunks, _sink_rows(sink_ref, layer, ctx_len, kvh), v_ext)
                store_heads(o, r0, ctx_len, kvh)


def _attention(sink, q, k, v, layer, n_lat_rows, seq, ctx_len, with_ctx):
    rows = q.shape[0]
    n_lat_steps = n_lat_rows // TQ
    steps = rows // TQ if with_ctx else n_lat_steps
    steps_per_seq = seq // TQ
    n_batch = n_lat_rows // seq
    hb = TQ // BLOCK
    last_blk = rows // BLOCK - 1

    main = lambda i: (i, 0)
    prev = lambda i: (jnp.maximum(i * hb - 1, 0), 0)
    nxt = lambda i: (jnp.minimum((i + 1) * hb, last_blk), 0)
    ctx = lambda i: (n_lat_rows // ctx_len + jnp.minimum(i // steps_per_seq, n_batch - 1), 0)
    kv_specs = [pl.BlockSpec((TQ, KV_W), main), pl.BlockSpec((BLOCK, KV_W), prev),
                pl.BlockSpec((BLOCK, KV_W), nxt)]
    return pl.pallas_call(
        functools.partial(_attn_kernel, layer=layer, n_lat_steps=n_lat_steps,
                          steps_per_seq=steps_per_seq, ctx_len=ctx_len),
        grid=(steps,),
        in_specs=[pl.BlockSpec(memory_space=pltpu.SMEM), pl.BlockSpec((TQ, ATTN_W), main)]
                 + kv_specs + kv_specs
                 + [pl.BlockSpec((ctx_len, KV_W), ctx), pl.BlockSpec((ctx_len, KV_W), ctx)],
        out_specs=pl.BlockSpec((TQ, ATTN_W), main),
        out_shape=jax.ShapeDtypeStruct((steps * TQ, ATTN_W), BF16),
        scratch_shapes=[pltpu.VMEM((TQ + 2 * BLOCK, KV_W), BF16),
                        pltpu.VMEM((TQ + 2 * BLOCK, KV_W), BF16)],
        compiler_params=_params(1),
        name="attention",
    )(sink, q, k, k, k, v, v, v, k, v)


CONV_SHIFT_PAD = 24


CONV_HALF = CONV_CH // 2
CONV_ROWS = 128


def _conv_tile(u_ref, up_ref, un_ref, has_prev, has_next, w_ref, b_ref, lg_ref, lb_ref,
               ext, ysh, cbuf, tc, store, side_work=()):
    n_sh = tc + CONV_SHIFT_PAD
    rc = min(tc, CONV_ROWS)
    first = CONV_HALO - CONV_K // 2
    side = list(side_work)

    def fill_ext():
        if up_ref is None:
            ext[0:CONV_HALO] = jnp.zeros((CONV_HALO, CONV_CH), F32)
        else:
            ext[0:CONV_HALO] = jnp.where(has_prev, up_ref[...], 0.0)
        ext[CONV_HALO:CONV_HALO + tc] = u_ref[...]
        ext[CONV_HALO + tc:2 * CONV_HALO + tc] = jnp.where(has_next, un_ref[...], 0.0)

    def shift_copies(h0):
        for s in range(1, SUBLANES):
            ysh[s - 1] = ext[s:s + n_sh, h0:h0 + CONV_HALF]

    def taps(h0, cc, work):
        l0 = cc * LANES
        c0 = h0 + l0
        if work is not None:
            work()
        for rb in range(tc // rc):
            acc = jnp.zeros((rc, LANES), F32)
            for tap in range(CONV_K):
                q8, s = divmod(first + tap, SUBLANES)
                r0 = SUBLANES * q8 + rb * rc
                if s == 0:
                    rows = ext[r0:r0 + rc, c0:c0 + LANES]
                else:
                    rows = ysh[s - 1, r0:r0 + rc, l0:l0 + LANES]
                acc = acc + rows * w_ref[tap:tap + 1, c0:c0 + LANES]
            cbuf[rb * rc:(rb + 1) * rc, c0:c0 + LANES] = acc + b_ref[:, c0:c0 + LANES]

    def norm_act():
        chunks = [slice(c0, c0 + NORM_COLS) for c0 in range(0, CONV_CH, NORM_COLS)]
        mu = jnp.mean(cbuf[...], axis=-1, keepdims=True)
        sq = [jnp.sum(jnp.square(cbuf[:, cols] - mu), axis=-1, keepdims=True) for cols in chunks]
        rs = lax.rsqrt(functools.reduce(lambda a, b: a + b, sq) * (1.0 / CONV_CH) + EPS)
        for cols in chunks:
            y = (cbuf[:, cols] - mu) * rs * lg_ref[:, cols] + lb_ref[:, cols]
            store(cols, (y * jax.nn.sigmoid(y)).astype(BF16))

    fill_ext()
    for half in range(CONV_CH // CONV_HALF):
        h0 = half * CONV_HALF
        shift_copies(h0)
        for cc in range(CONV_HALF // LANES):
            taps(h0, cc, side.pop(0) if side else None)
    for work in side:
        work()
    norm_act()


def _conv_scratch(tc):
    return [pltpu.VMEM((tc + 2 * CONV_HALO, CONV_CH), F32),
            pltpu.VMEM((SUBLANES - 1, tc + CONV_SHIFT_PAD, CONV_HALF), F32),
            pltpu.VMEM((tc, CONV_CH), F32)]


def _conv_kernel(u_ref, up_ref, un_ref, w_ref, b_ref, lg_ref, lb_ref, o_ref,
                 ext, ysh, cbuf, *, tc, tiles_per_seq):
    pos = pl.program_id(0) % tiles_per_seq

    def store(cols, val):
        o_ref[:, cols] = val

    _conv_tile(u_ref, up_ref, un_ref, pos != 0, pos != tiles_per_seq - 1,
               w_ref, b_ref, lg_ref, lb_ref, ext, ysh, cbuf, tc, store)


def _conv_call(u, w, b, lg, lb, layer, tc, row0, n_rows, tiles_per_seq):
    rows = u.shape[0]
    t0 = row0 // tc
    hb = tc // CONV_HALO
    last_h = rows // CONV_HALO - 1
    return pl.pallas_call(
        functools.partial(_conv_kernel, tc=tc, tiles_per_seq=tiles_per_seq),
        grid=(n_rows // tc,),
        in_specs=[
            pl.BlockSpec((tc, CONV_CH), lambda i: (t0 + i, 0)),
            pl.BlockSpec((CONV_HALO, CONV_CH), lambda i: (jnp.maximum((t0 + i) * hb - 1, 0), 0)),
            pl.BlockSpec((CONV_HALO, CONV_CH), lambda i: (jnp.minimum((t0 + i + 1) * hb, last_h), 0)),
            _layer_block((CONV_K, CONV_CH), layer, 4),
            _layer_block((1, CONV_CH), layer, 4),
            _layer_block((1, CONV_CH), layer, 4),
            _layer_block((1, CONV_CH), layer, 4),
        ],
        out_specs=pl.BlockSpec((tc, CONV_CH), lambda i: (i, 0)),
        out_shape=jax.ShapeDtypeStruct((n_rows, CONV_CH), BF16),
        scratch_shapes=_conv_scratch(tc),
        compiler_params=_params(1),
        name="conv",
    )(u, u, u, w, b, lg, lb)


OUT_CHUNK = 256


def _outproj_chunk(c, att_ref, load_cv, x_ref, mod_ref, w_ref, o_ref):
    cols = slice(c * OUT_CHUNK, (c + 1) * OUT_CHUNK)
    mix = jnp.concatenate([att_ref[...], load_cv()], axis=1)
    r = jnp.dot(mix, w_ref[:, cols], preferred_element_type=F32)
    g1 = mod_ref[:, 2 * D_MODEL + c * OUT_CHUNK:2 * D_MODEL + (c + 1) * OUT_CHUNK]
    o_ref[:, cols] = x_ref[:, cols] + g1 * r


def _outproj_kernel(att_ref, cv_ref, x_ref, mod_ref, w_ref, o_ref):
    for c in range(D_MODEL // OUT_CHUNK):
        _outproj_chunk(c, att_ref, lambda: cv_ref[...], x_ref, mod_ref, w_ref, o_ref)


def _outproj(att, cv, xa, row0, n_rows, mod, w, layer, n_lat_rows, seq):
    tm = TM_OUT
    n_batch = n_lat_rows // seq
    t0 = row0 // tm
    row = lambda i: (t0 + i, 0)
    return pl.pallas_call(
        _outproj_kernel,
        grid=(n_rows // tm,),
        in_specs=[
            pl.BlockSpec((tm, ATTN_W), row),
            pl.BlockSpec((tm, CONV_CH), lambda i: (i, 0)),
            pl.BlockSpec((tm, D_MODEL), row),
            _mod_spec(layer, tm, seq, n_batch, t0),
            _layer_block((MIX_W, D_MODEL), layer, 2),
        ],
        out_specs=pl.BlockSpec((tm, D_MODEL), row),
        out_shape=jax.ShapeDtypeStruct(xa.shape, F32),
        input_output_aliases={2: 0},
        compiler_params=_params(1),
        name="outproj",
    )(att, cv, xa, mod, w)


def _outproj_conv_kernel(att_ref, x_ref, mod_ref, w_ref, u0_ref, u0n_ref, un_ref, unp_ref, unn_ref,
                         cw_ref, cb_ref, lg_ref, lb_ref, o_ref, cvbuf, ext, ysh, cbuf,
                         *, tc, tiles_per_seq):
    i = pl.program_id(0)
    conv_refs = (cw_ref, cb_ref, lg_ref, lb_ref, ext, ysh, cbuf, tc)

    def store_first(cols, val):
        cvbuf[0, :, cols] = val

    @pl.when(i == 0)
    def _first():
        _conv_tile(u0_ref, None, u0n_ref, False, tiles_per_seq > 1, *conv_refs, store_first)

    nxt = jnp.minimum(i + 1, pl.num_programs(0) - 1)
    pos = nxt % tiles_per_seq
    slot = i % 2

    def store_next(cols, val):
        cvbuf[1 - slot, :, cols] = val

    chunks = [functools.partial(_outproj_chunk, c, att_ref, lambda: cvbuf[slot], x_ref, mod_ref, w_ref, o_ref)
              for c in range(D_MODEL // OUT_CHUNK)]
    _conv_tile(un_ref, unp_ref, unn_ref, pos != 0, pos != tiles_per_seq - 1, *conv_refs, store_next,
               side_work=chunks)


def _outproj_conv(att, u, res, mod, w, cw, cb, lg, lb, layer, out_rows, n_lat_rows, seq):
    tm = TM_OUT
    n_tiles = n_lat_rows // tm
    tiles_per_seq = seq // tm
    n_batch = n_lat_rows // seq
    hb = tm // CONV_HALO
    row = lambda i: (i, 0)
    nxt = lambda i: jnp.minimum(i + 1, n_tiles - 1)
    return pl.pallas_call(
        functools.partial(_outproj_conv_kernel, tc=tm, tiles_per_seq=tiles_per_seq),
        grid=(n_tiles,),
        in_specs=[
            pl.BlockSpec((tm, ATTN_W), row),
            pl.BlockSpec((tm, D_MODEL), row),
            _mod_spec(layer, tm, seq, n_batch),
            _layer_block((MIX_W, D_MODEL), layer, 2),
            pl.BlockSpec((tm, CONV_CH), lambda i: (0, 0), pipeline_mode=pl.Buffered(1)),
            pl.BlockSpec((CONV_HALO, CONV_CH), lambda i: (hb, 0)),
            pl.BlockSpec((tm, CONV_CH), lambda i: (nxt(i), 0)),
            pl.BlockSpec((CONV_HALO, CONV_CH), lambda i: (nxt(i) * hb - 1, 0)),
            pl.BlockSpec((CONV_HALO, CONV_CH), lambda i: ((nxt(i) + 1) * hb, 0)),
            _layer_block((CONV_K, CONV_CH), layer, 4),
            _layer_block((1, CONV_CH), layer, 4),
            _layer_block((1, CONV_CH), layer, 4),
            _layer_block((1, CONV_CH), layer, 4),
        ],
        out_specs=pl.BlockSpec((tm, D_MODEL), row),
        out_shape=jax.ShapeDtypeStruct((out_rows, D_MODEL), F32),
        scratch_shapes=[pltpu.VMEM((2, tm, CONV_CH), BF16)] + _conv_scratch(tm),
        input_output_aliases={1: 0} if res.shape[0] == out_rows else {},
        compiler_params=_params(1),
        name="outproj_conv",
    )(att, res, mod, w, u, u, u, u, u, cw, cb, lg, lb)


def _mlp_kernel(x_ref, mod_ref, g_ref, gf_ref, w1_ref, w2_ref, o_ref, h_s, *, final):
    j = pl.program_id(1)

    @pl.when(j == 0)
    def _prologue():
        _rms_modulate_to(h_s, x_ref, g_ref[...],
                         mod_ref[:, 3 * D_MODEL:4 * D_MODEL], mod_ref[:, 4 * D_MODEL:5 * D_MODEL])
        o_ref[...] = jnp.zeros_like(o_ref)

    a = jnp.dot(h_s[...], w1_ref[...], preferred_element_type=F32)
    a = jnp.maximum(a, 0.0)
    o_ref[...] += jnp.dot((a * a).astype(BF16), w2_ref[...], preferred_element_type=F32)

    @pl.when(j == pl.num_programs(1) - 1)
    def _epilogue():
        g2 = mod_ref[:, 5 * D_MODEL:6 * D_MODEL]
        if not final:
            o_ref[...] = x_ref[...] + g2 * o_ref[...]
        else:
            for r0 in range(0, o_ref.shape[0], NORM_ROWS):
                rows = slice(r0, r0 + NORM_ROWS)
                y = x_ref[rows, :] + g2 * o_ref[rows, :]
                rs = lax.rsqrt(jnp.mean(y * y, axis=-1, keepdims=True) + EPS)
                o_ref[rows, :] = y * rs * gf_ref[...]


def _mlp(xa, mod, g, gf, w1, w2, layer, n_lat_rows, seq, final):
    rows = xa.shape[0]
    tm, fc = TM_MLP, FC_MLP
    n_batch = n_lat_rows // seq
    row = lambda i, j: (i, 0)
    return pl.pallas_call(
        functools.partial(_mlp_kernel, final=final),
        grid=(rows // tm, D_FF // fc),
        in_specs=[
            pl.BlockSpec((tm, D_MODEL), row),
            _mod_spec(layer, tm, seq, n_batch),
            _layer_block((1, D_MODEL), layer, 4),
            pl.BlockSpec((1, D_MODEL), lambda i, j: (0, 0)),
            pl.BlockSpec((None, D_MODEL, fc), lambda i, j: (layer, 0, j)),
            pl.BlockSpec((None, fc, D_MODEL), lambda i, j: (layer, j, 0)),
        ],
        out_specs=pl.BlockSpec((tm, D_MODEL), row),
        out_shape=jax.ShapeDtypeStruct((rows, D_MODEL), F32),
        scratch_shapes=[pltpu.VMEM((tm, D_MODEL), BF16)],
        input_output_aliases={0: 0},
        compiler_params=_params(2),
        name="mlp",
    )(xa, mod, g, gf, w1, w2)


def _rope_tables(seq, pad_rows):
    t = jnp.arange(seq, dtype=jnp.int32)
    row = (t // GRID_W).astype(F32)
    col = (t % GRID_W).astype(F32)
    n_freq = HEAD_DIM // 4
    inv = ROPE_BASE ** (-jnp.arange(n_freq, dtype=F32) / n_freq)
    theta = jnp.concatenate([row[:, None] * inv, col[:, None] * inv], axis=-1)
    theta = jnp.concatenate([theta, theta], axis=-1)
    sign = jnp.concatenate([-jnp.ones((HEAD_DIM // 2,), F32), jnp.ones((HEAD_DIM // 2,), F32)])
    cos = jnp.concatenate([jnp.cos(theta), jnp.ones((pad_rows, HEAD_DIM), F32)], axis=0)
    sin = jnp.concatenate([jnp.sin(theta) * sign, jnp.zeros((pad_rows, HEAD_DIM), F32)], axis=0)
    return cos, sin


def kernel(x, c, ctx, c_ctx, w_ada, b_ada, g_mix, g_mlp, w_in, attn_sink, conv_w, conv_b,
           conv_ln_g, conv_ln_b, w_out, w_mlp1, w_mlp2, g_final):
    n_batch, seq, d = x.shape
    ctx_len = ctx.shape[1]
    n_lat_rows = n_batch * seq
    assert d == D_MODEL and n_batch + 1 <= MOD_ROWS
    assert seq % TM_MLP == 0 and (n_batch * ctx_len) % TM_MLP == 0 and TQ % ctx_len == 0

    n_ctx_rows = n_batch * ctx_len
    rows = n_lat_rows + n_ctx_rows
    x2 = x.reshape(n_lat_rows, d)
    ctx2 = ctx.reshape(n_ctx_rows, d)
    cc = jnp.concatenate([c, c_ctx[None], jnp.zeros((MOD_ROWS - n_batch - 1, d), F32)], axis=0)
    mod = _ada_table(cc, w_ada, b_ada).reshape(DEPTH, MOD_ROWS, 1, 6 * d)
    cos_t, sin_t = _rope_tables(seq, TM_IN)

    w_in_b = w_in.astype(BF16)
    w_out_b = w_out.astype(BF16)
    w1_b = w_mlp1.astype(BF16)
    w2_b = w_mlp2.astype(BF16)
    g_mix3 = g_mix.reshape(DEPTH, 1, d)
    g_mlp3 = g_mlp.reshape(DEPTH, 1, d)
    gf = g_final.reshape(1, d)
    conv_b3 = conv_b.reshape(DEPTH, 1, CONV_CH)
    conv_lg3 = conv_ln_g.reshape(DEPTH, 1, CONV_CH)
    conv_lb3 = conv_ln_b.reshape(DEPTH, 1, CONV_CH)

    srcs = (x2, ctx2)
    for l in range(DEPTH):
        last = l == DEPTH - 1
        q, k, v, u, *joined = _inproj(srcs, mod, g_mix3, cos_t, sin_t, w_in_b, l, rows, n_lat_rows, seq)
        xa = joined[0] if joined else srcs[0]
        att = _attention(attn_sink, q, k, v, l, n_lat_rows, seq, ctx_len, with_ctx=not last)
        conv_args = (conv_w, conv_b3, conv_lg3, conv_lb3, l)
        xa = _outproj_conv(att, u, xa, mod, w_out_b, *conv_args,
                           n_lat_rows if last else rows, n_lat_rows, seq)
        if not last:
            cv_ctx = _conv_call(u, *conv_args, ctx_len, n_lat_rows, n_ctx_rows, 1)
            xa = _outproj(att, cv_ctx, xa, n_lat_rows, n_ctx_rows, mod, w_out_b, l, n_lat_rows, seq)
        xa = _mlp(xa, mod, g_mlp3, gf, w1_b, w2_b, l, n_lat_rows, seq, last)
        srcs = (xa,)
    return xa.reshape(n_batch, seq, d)
```

```python
import functools
import math

import jax
import jax.numpy as jnp
from jax import lax
from jax.experimental import pallas as pl
from jax.experimental.pallas import tpu as pltpu

F32 = jnp.float32
BF16 = jnp.bfloat16

D_MODEL = 2048
DEPTH = 4
GRID_W = 64
HEAD_DIM = 128
ATTN_HEADS = 8
KV_HEADS = 2
GQ = ATTN_HEADS // KV_HEADS
ATTN_W = ATTN_HEADS * HEAD_DIM
KV_W = KV_HEADS * HEAD_DIM
BLOCK = 128
ROPE_BASE = 10000.0
CONV_CH = D_MODEL // 2
CONV_K = 31
CONV_HALO = 16
MIX_W = ATTN_W + CONV_CH
IN_W = ATTN_W + 2 * KV_W + 2 * CONV_CH
D_FF = 4 * D_MODEL
EPS = 1e-6
NEG = -1e30

SUBLANES = 8
LANES = 128
V7X_VMEM_LIMIT = 56 * 1024 * 1024
MOD_ROWS = 8

TM_IN = 512
TQ = 512
TM_OUT = 512
TM_MLP = 1024
FC_MLP = 512
TN_ADA = 1024


def _params(n_axes, vmem=V7X_VMEM_LIMIT):
    return pltpu.CompilerParams(dimension_semantics=("arbitrary",) * n_axes,
                                vmem_limit_bytes=vmem)


SINGLE_BUFFER_BYTES = 1 << 20


def _layer_block(shape, layer, itemsize):
    zeros = (0,) * len(shape)
    big = math.prod(shape) * itemsize >= SINGLE_BUFFER_BYTES
    mode = dict(pipeline_mode=pl.Buffered(1)) if big else {}
    return pl.BlockSpec((None,) + tuple(shape), lambda *_: (layer,) + zeros, **mode)


def _mod_spec(layer, tm, seq, n_batch, t0=0):
    return pl.BlockSpec((None, None, 1, 6 * D_MODEL),
                        lambda i, *_: (layer, jnp.minimum((t0 + i) * tm // seq, n_batch), 0, 0))


NORM_ROWS = 16
NORM_COLS = 256


def _rms_modulate_to(h_s, x_ref, g, shift, scale):
    gain = g * (1.0 + scale)
    for r0 in range(0, x_ref.shape[0], NORM_ROWS):
        rows = slice(r0, r0 + NORM_ROWS)
        x = x_ref[rows, :]
        rs = lax.rsqrt(jnp.mean(x * x, axis=-1, keepdims=True) + EPS)
        h_s[rows, :] = (x * rs * gain + shift).astype(BF16)


def _ada_kernel(c_ref, w_ref, b_ref, o_ref):
    c = c_ref[...]
    s = (c * jax.nn.sigmoid(c)).astype(BF16)
    o_ref[...] = jnp.dot(s, w_ref[...].astype(BF16), preferred_element_type=F32) + b_ref[...]


def _ada_table(cc, w_ada, b_ada):
    n_out = 6 * D_MODEL
    return pl.pallas_call(
        _ada_kernel,
        grid=(DEPTH, n_out // TN_ADA),
        in_specs=[
            pl.BlockSpec((MOD_ROWS, D_MODEL), lambda l, j: (0, 0)),
            pl.BlockSpec((None, D_MODEL, TN_ADA), lambda l, j: (l, 0, j)),
            pl.BlockSpec((None, 1, TN_ADA), lambda l, j: (l, 0, j)),
        ],
        out_specs=pl.BlockSpec((None, MOD_ROWS, TN_ADA), lambda l, j: (l, 0, j)),
        out_shape=jax.ShapeDtypeStruct((DEPTH, MOD_ROWS, n_out), F32),
        compiler_params=_params(2),
        name="ada_table",
    )(cc, w_ada, b_ada.reshape(DEPTH, 1, n_out))


def _inproj_kernel(*refs, n_lat_tiles, split_src):
    if split_src:
        xl_ref, xc_ref, mod_ref, g_ref, cos_ref, sin_ref, w_ref, q_ref, k_ref, v_ref, u_ref, xa_ref, h_s = refs
        xa_ref[...] = jnp.where(pl.program_id(0) < n_lat_tiles, xl_ref[...], xc_ref[...])
        x_ref = xa_ref
    else:
        x_ref, mod_ref, g_ref, cos_ref, sin_ref, w_ref, q_ref, k_ref, v_ref, u_ref, h_s = refs
    _rms_modulate_to(h_s, x_ref, g_ref[...], mod_ref[:, 0:D_MODEL], mod_ref[:, D_MODEL:2 * D_MODEL])
    cos = cos_ref[...]
    sin = sin_ref[...]

    def rope(seg):
        return seg * cos + pltpu.roll(seg, HEAD_DIM // 2, 1) * sin

    nq = 512
    for c in range(ATTN_W // nq):
        r = jnp.dot(h_s[...], w_ref[:, c * nq:(c + 1) * nq], preferred_element_type=F32)
        for hh in range(nq // HEAD_DIM):
            lo = hh * HEAD_DIM
            q_ref[:, c * nq + lo:c * nq + lo + HEAD_DIM] = rope(r[:, lo:lo + HEAD_DIM]).astype(BF16)
    r = jnp.dot(h_s[...], w_ref[:, ATTN_W:ATTN_W + 2 * KV_W], preferred_element_type=F32)
    for hh in range(KV_HEADS):
        lo = hh * HEAD_DIM
        k_ref[:, lo:lo + HEAD_DIM] = rope(r[:, lo:lo + HEAD_DIM]).astype(BF16)
    v_ref[...] = r[:, KV_W:2 * KV_W].astype(BF16)
    nu = 256
    u0 = ATTN_W + 2 * KV_W
    for c in range(CONV_CH // nu):
        a = jnp.dot(h_s[...], w_ref[:, u0 + c * nu:u0 + (c + 1) * nu], preferred_element_type=F32)
        gt = jnp.dot(h_s[...], w_ref[:, u0 + CONV_CH + c * nu:u0 + CONV_CH + (c + 1) * nu],
                     preferred_element_type=F32)
        u_ref[:, c * nu:(c + 1) * nu] = a * jax.nn.sigmoid(gt)


def _inproj(srcs, mod, g, cos_t, sin_t, w, layer, rows, n_lat_rows, seq):
    tm = TM_IN
    n_lat = n_lat_rows // tm
    per_seq = seq // tm
    n_batch = n_lat_rows // seq
    split_src = len(srcs) == 2

    def rope_idx(i):
        return (jnp.where(i < n_lat, i % per_seq, per_seq), 0)

    row = lambda i: (i, 0)
    if split_src:
        src_specs = [pl.BlockSpec((tm, D_MODEL), lambda i: (jnp.minimum(i, n_lat - 1), 0)),
                     pl.BlockSpec((tm, D_MODEL), lambda i: (jnp.maximum(i - n_lat, 0), 0))]
    else:
        src_specs = [pl.BlockSpec((tm, D_MODEL), row)]
    out_specs = [pl.BlockSpec((tm, ATTN_W), row), pl.BlockSpec((tm, KV_W), row),
                 pl.BlockSpec((tm, KV_W), row), pl.BlockSpec((tm, CONV_CH), row)]
    out_shape = [jax.ShapeDtypeStruct((rows, ATTN_W), BF16), jax.ShapeDtypeStruct((rows, KV_W), BF16),
                 jax.ShapeDtypeStruct((rows, KV_W), BF16), jax.ShapeDtypeStruct((rows, CONV_CH), F32)]
    if split_src:
        out_specs.append(pl.BlockSpec((tm, D_MODEL), row))
        out_shape.append(jax.ShapeDtypeStruct((rows, D_MODEL), F32))
    return pl.pallas_call(
        functools.partial(_inproj_kernel, n_lat_tiles=n_lat, split_src=split_src),
        grid=(rows // tm,),
        in_specs=src_specs + [
            _mod_spec(layer, tm, seq, n_batch),
            _layer_block((1, D_MODEL), layer, 4),
            pl.BlockSpec((tm, HEAD_DIM), rope_idx),
            pl.BlockSpec((tm, HEAD_DIM), rope_idx),
            _layer_block((D_MODEL, IN_W), layer, 2),
        ],
        out_specs=out_specs,
        out_shape=out_shape,
        scratch_shapes=[pltpu.VMEM((tm, D_MODEL), BF16)],
        compiler_params=_params(1),
        name="inproj",
    )(*srcs, mod, g, cos_t, sin_t, w)


SM_SCALE = HEAD_DIM ** -0.5
EXP2_SCALE = SM_SCALE * math.log2(math.e)


def _nt_dot(a, b):
    return lax.dot_general(a, b, (((1,), (1,)), ((), ())), preferred_element_type=F32)


def _stack_heads(q_ref, r0, nrows, kvh):
    return jnp.concatenate(
        [q_ref[r0:r0 + nrows, (kvh * GQ + g) * HEAD_DIM:(kvh * GQ + g + 1) * HEAD_DIM] for g in range(GQ)],
        axis=0)


def _sink_rows(sink_ref, layer, nrows, kvh):
    return jnp.concatenate(
        [jnp.full((nrows, LANES), sink_ref[layer, kvh * GQ + g] * (1.0 / SM_SCALE), F32)
         for g in range(GQ)], axis=0)


def _softmax_pv(chunks, sink, v_ext):
    mx = chunks[0]
    for ch in chunks[1:]:
        mx = jnp.maximum(mx, ch)
    m = jnp.maximum(jnp.max(mx, axis=-1, keepdims=True), sink)
    e = jnp.concatenate([jnp.exp2((ch - m) * EXP2_SCALE).astype(BF16) for ch in chunks], axis=1)
    o = jnp.dot(e, v_ext, preferred_element_type=F32)
    den = o[:, HEAD_DIM:] + jnp.exp2((sink - m) * EXP2_SCALE)
    return o[:, :HEAD_DIM] / den


def _attn_kernel(sink_ref, q_ref, km_ref, kp_ref, kn_ref, vm_ref, vp_ref, vn_ref, kc_ref, vc_ref,
                 o_ref, kbuf, vbuf, *, layer, n_lat_steps, steps_per_seq, ctx_len):
    i = pl.program_id(0)
    win = 3 * BLOCK
    nblk = TQ // BLOCK

    def store_heads(o, r0, nrows, kvh):
        for g in range(GQ):
            c0 = (kvh * GQ + g) * HEAD_DIM
            o_ref[r0:r0 + nrows, c0:c0 + HEAD_DIM] = o[g * nrows:(g + 1) * nrows].astype(BF16)

    @pl.when(i < n_lat_steps)
    def _latent():
        kbuf[0:BLOCK] = kp_ref[...]
        kbuf[BLOCK:BLOCK + TQ] = km_ref[...]
        kbuf[BLOCK + TQ:2 * BLOCK + TQ] = kn_ref[...]
        vbuf[0:BLOCK] = vp_ref[...]
        vbuf[BLOCK:BLOCK + TQ] = vm_ref[...]
        vbuf[BLOCK + TQ:2 * BLOCK + TQ] = vn_ref[...]
        pos = i % steps_per_seq
        off_first = jnp.where(pos == 0, BLOCK, 0)
        off_last = jnp.where(pos == steps_per_seq - 1, BLOCK, 0)
        r = lax.broadcasted_iota(jnp.int32, (GQ * BLOCK, LANES), 0) & (BLOCK - 1)
        j = lax.broadcasted_iota(jnp.int32, (GQ * BLOCK, LANES), 1)
        ones = jnp.ones((win + ctx_len, HEAD_DIM), BF16)

        def scores(t):
            blk, kvh = divmod(t, KV_HEADS)
            r0, c0 = blk * BLOCK, kvh * HEAD_DIM
            qg = _stack_heads(q_ref, r0, BLOCK, kvh)
            keys = jnp.concatenate([kbuf[r0:r0 + win, c0:c0 + HEAD_DIM], kc_ref[:, c0:c0 + HEAD_DIM]],
                                   axis=0)
            return _nt_dot(qg, keys)

        def finish(t, s):
            blk, kvh = divmod(t, KV_HEADS)
            r0, c0 = blk * BLOCK, kvh * HEAD_DIM
            lo = r + off_first if blk == 0 else r
            hi = r - off_last if blk == nblk - 1 else r
            chunks = [s[:, n * LANES:(n + 1) * LANES] for n in range((win + ctx_len) // LANES)]
            chunks[0] = jnp.where(j >= lo, chunks[0], NEG)
            chunks[2] = jnp.where(j <= hi, chunks[2], NEG)
            vals = jnp.concatenate([vbuf[r0:r0 + win, c0:c0 + HEAD_DIM], vc_ref[:, c0:c0 + HEAD_DIM]],
                                   axis=0)
            v_ext = jnp.concatenate([vals, ones], axis=1)
            o = _softmax_pv(chunks, _sink_rows(sink_ref, layer, BLOCK, kvh), v_ext)
            store_heads(o, r0, BLOCK, kvh)

        n_items = nblk * KV_HEADS
        s_next = scores(0)
        for t in range(n_items):
            s_cur = s_next
            if t + 1 < n_items:
                s_next = scores(t + 1)
            finish(t, s_cur)

    @pl.when(i >= n_lat_steps)
    def _context():
        ones = jnp.ones((ctx_len, HEAD_DIM), BF16)
        for bb in range(TQ // ctx_len):
            r0 = bb * ctx_len
            for kvh in range(KV_HEADS):
                c0 = kvh * HEAD_DIM
                qg = _stack_heads(q_ref, r0, ctx_len, kvh)
                s = _nt_dot(qg, km_ref[r0:r0 + ctx_len, c0:c0 + HEAD_DIM])
                chunks = [s[:, n * LANES:(n + 1) * LANES] for n in range(ctx_len // LANES)]
                v_ext = jnp.concatenate([vm_ref[r0:r0 + ctx_len, c0:c0 + HEAD_DIM], ones], axis=1)
                o = _softmax_pv(chunks, _sink_rows(sink_ref, layer, ctx_len, kvh), v_ext)
                store_heads(o, r0, ctx_len, kvh)


def _attention(sink, q, k, v, layer, n_lat_rows, seq, ctx_len, with_ctx):
    rows = q.shape[0]
    n_lat_steps = n_lat_rows // TQ
    steps = rows // TQ if with_ctx else n_lat_steps
    steps_per_seq = seq // TQ
    n_batch = n_lat_rows // seq
    hb = TQ // BLOCK
    last_blk = rows // BLOCK - 1

    main = lambda i: (i, 0)
    prev = lambda i: (jnp.maximum(i * hb - 1, 0), 0)
    nxt = lambda i: (jnp.minimum((i + 1) * hb, last_blk), 0)
    ctx = lambda i: (n_lat_rows // ctx_len + jnp.minimum(i // steps_per_seq, n_batch - 1), 0)
    kv_specs = [pl.BlockSpec((TQ, KV_W), main), pl.BlockSpec((BLOCK, KV_W), prev),
                pl.BlockSpec((BLOCK, KV_W), nxt)]
    return pl.pallas_call(
        functools.partial(_attn_kernel, layer=layer, n_lat_steps=n_lat_steps,
                          steps_per_seq=steps_per_seq, ctx_len=ctx_len),
        grid=(steps,),
        in_specs=[pl.BlockSpec(memory_space=pltpu.SMEM), pl.BlockSpec((TQ, ATTN_W), main)]
                 + kv_specs + kv_specs
                 + [pl.BlockSpec((ctx_len, KV_W), ctx), pl.BlockSpec((ctx_len, KV_W), ctx)],
        out_specs=pl.BlockSpec((TQ, ATTN_W), main),
        out_shape=jax.ShapeDtypeStruct((steps * TQ, ATTN_W), BF16),
        scratch_shapes=[pltpu.VMEM((TQ + 2 * BLOCK, KV_W), BF16),
                        pltpu.VMEM((TQ + 2 * BLOCK, KV_W), BF16)],
        compiler_params=_params(1),
        name="attention",
    )(sink, q, k, k, k, v, v, v, k, v)


CONV_SHIFT_PAD = 24


CONV_HALF = CONV_CH // 2
CONV_ROWS = 128


def _conv_tile(u_ref, up_ref, un_ref, has_prev, has_next, w_ref, b_ref, lg_ref, lb_ref,
               ext, ysh, cbuf, tc, store):
    n_sh = tc + CONV_SHIFT_PAD
    rc = min(tc, CONV_ROWS)
    first = CONV_HALO - CONV_K // 2

    def fill_ext():
        if up_ref is None:
            ext[0:CONV_HALO] = jnp.zeros((CONV_HALO, CONV_CH), F32)
        else:
            ext[0:CONV_HALO] = jnp.where(has_prev, up_ref[...], 0.0)
        ext[CONV_HALO:CONV_HALO + tc] = u_ref[...]
        ext[CONV_HALO + tc:2 * CONV_HALO + tc] = jnp.where(has_next, un_ref[...], 0.0)

    def shift_copies(h0):
        for s in range(1, SUBLANES):
            ysh[s - 1] = ext[s:s + n_sh, h0:h0 + CONV_HALF]

    def taps(h0, cc):
        l0 = cc * LANES
        c0 = h0 + l0
        for rb in range(tc // rc):
            acc = jnp.zeros((rc, LANES), F32)
            for tap in range(CONV_K):
                q8, s = divmod(first + tap, SUBLANES)
                r0 = SUBLANES * q8 + rb * rc
                if s == 0:
                    rows = ext[r0:r0 + rc, c0:c0 + LANES]
                else:
                    rows = ysh[s - 1, r0:r0 + rc, l0:l0 + LANES]
                acc = acc + rows * w_ref[tap:tap + 1, c0:c0 + LANES]
            cbuf[rb * rc:(rb + 1) * rc, c0:c0 + LANES] = acc + b_ref[:, c0:c0 + LANES]

    def norm_act():
        chunks = [slice(c0, c0 + NORM_COLS) for c0 in range(0, CONV_CH, NORM_COLS)]
        mu = jnp.mean(cbuf[...], axis=-1, keepdims=True)
        sq = [jnp.sum(jnp.square(cbuf[:, cols] - mu), axis=-1, keepdims=True) for cols in chunks]
        rs = lax.rsqrt(functools.reduce(lambda a, b: a + b, sq) * (1.0 / CONV_CH) + EPS)
        for cols in chunks:
            y = (cbuf[:, cols] - mu) * rs * lg_ref[:, cols] + lb_ref[:, cols]
            store(cols, (y * jax.nn.sigmoid(y)).astype(BF16))

    fill_ext()
    for half in range(CONV_CH // CONV_HALF):
        h0 = half * CONV_HALF
        shift_copies(h0)
        for cc in range(CONV_HALF // LANES):
            taps(h0, cc)
    norm_act()


def _conv_scratch(tc):
    return [pltpu.VMEM((tc + 2 * CONV_HALO, CONV_CH), F32),
            pltpu.VMEM((SUBLANES - 1, tc + CONV_SHIFT_PAD, CONV_HALF), F32),
            pltpu.VMEM((tc, CONV_CH), F32)]


def _conv_kernel(u_ref, up_ref, un_ref, w_ref, b_ref, lg_ref, lb_ref, o_ref,
                 ext, ysh, cbuf, *, tc, tiles_per_seq):
    pos = pl.program_id(0) % tiles_per_seq

    def store(cols, val):
        o_ref[:, cols] = val

    _conv_tile(u_ref, up_ref, un_ref, pos != 0, pos != tiles_per_seq - 1,
               w_ref, b_ref, lg_ref, lb_ref, ext, ysh, cbuf, tc, store)


def _conv_call(u, w, b, lg, lb, layer, tc, row0, n_rows, tiles_per_seq):
    rows = u.shape[0]
    t0 = row0 // tc
    hb = tc // CONV_HALO
    last_h = rows // CONV_HALO - 1
    return pl.pallas_call(
        functools.partial(_conv_kernel, tc=tc, tiles_per_seq=tiles_per_seq),
        grid=(n_rows // tc,),
        in_specs=[
            pl.BlockSpec((tc, CONV_CH), lambda i: (t0 + i, 0)),
            pl.BlockSpec((CONV_HALO, CONV_CH), lambda i: (jnp.maximum((t0 + i) * hb - 1, 0), 0)),
            pl.BlockSpec((CONV_HALO, CONV_CH), lambda i: (jnp.minimum((t0 + i + 1) * hb, last_h), 0)),
            _layer_block((CONV_K, CONV_CH), layer, 4),
            _layer_block((1, CONV_CH), layer, 4),
            _layer_block((1, CONV_CH), layer, 4),
            _layer_block((1, CONV_CH), layer, 4),
        ],
        out_specs=pl.BlockSpec((tc, CONV_CH), lambda i: (i, 0)),
        out_shape=jax.ShapeDtypeStruct((n_rows, CONV_CH), BF16),
        scratch_shapes=_conv_scratch(tc),
        compiler_params=_params(1),
        name="conv",
    )(u, u, u, w, b, lg, lb)


OUT_CHUNK = 256


def _outproj_chunk(c, att_ref, load_cv, x_ref, mod_ref, w_ref, o_ref):
    cols = slice(c * OUT_CHUNK, (c + 1) * OUT_CHUNK)
    mix = jnp.concatenate([att_ref[...], load_cv()], axis=1)
    r = jnp.dot(mix, w_ref[:, cols], preferred_element_type=F32)
    g1 = mod_ref[:, 2 * D_MODEL + c * OUT_CHUNK:2 * D_MODEL + (c + 1) * OUT_CHUNK]
    o_ref[:, cols] = x_ref[:, cols] + g1 * r


def _outproj_kernel(att_ref, cv_ref, x_ref, mod_ref, w_ref, o_ref):
    for c in range(D_MODEL // OUT_CHUNK):
        _outproj_chunk(c, att_ref, lambda: cv_ref[...], x_ref, mod_ref, w_ref, o_ref)


def _outproj(att, cv, xa, row0, n_rows, mod, w, layer, n_lat_rows, seq):
    tm = TM_OUT
    n_batch = n_lat_rows // seq
    t0 = row0 // tm
    row = lambda i: (t0 + i, 0)
    return pl.pallas_call(
        _outproj_kernel,
        grid=(n_rows // tm,),
        in_specs=[
            pl.BlockSpec((tm, ATTN_W), row),
            pl.BlockSpec((tm, CONV_CH), lambda i: (i, 0)),
            pl.BlockSpec((tm, D_MODEL), row),
            _mod_spec(layer, tm, seq, n_batch, t0),
            _layer_block((MIX_W, D_MODEL), layer, 2),
        ],
        out_specs=pl.BlockSpec((tm, D_MODEL), row),
        out_shape=jax.ShapeDtypeStruct(xa.shape, F32),
        input_output_aliases={2: 0},
        compiler_params=_params(1),
        name="outproj",
    )(att, cv, xa, mod, w)


def _outproj_conv_kernel(att_ref, x_ref, mod_ref, w_ref, u0_ref, u0n_ref, un_ref, unp_ref, unn_ref,
                         cw_ref, cb_ref, lg_ref, lb_ref, o_ref, cvbuf, ext, ysh, cbuf,
                         *, tc, tiles_per_seq):
    i = pl.program_id(0)
    conv_refs = (cw_ref, cb_ref, lg_ref, lb_ref, ext, ysh, cbuf, tc)

    def store_first(cols, val):
        cvbuf[0, :, cols] = val

    @pl.when(i == 0)
    def _first():
        _conv_tile(u0_ref, None, u0n_ref, False, tiles_per_seq > 1, *conv_refs, store_first)

    nxt = jnp.minimum(i + 1, pl.num_programs(0) - 1)
    pos = nxt % tiles_per_seq
    slot = i % 2

    def store_next(cols, val):
        cvbuf[1 - slot, :, cols] = val

    for c in range(D_MODEL // OUT_CHUNK):
        _outproj_chunk(c, att_ref, lambda: cvbuf[slot], x_ref, mod_ref, w_ref, o_ref)
    _conv_tile(un_ref, unp_ref, unn_ref, pos != 0, pos != tiles_per_seq - 1, *conv_refs, store_next)


def _outproj_conv(att, u, res, mod, w, cw, cb, lg, lb, layer, out_rows, n_lat_rows, seq):
    tm = TM_OUT
    n_tiles = n_lat_rows // tm
    tiles_per_seq = seq // tm
    n_batch = n_lat_rows // seq
    hb = tm // CONV_HALO
    row = lambda i: (i, 0)
    nxt = lambda i: jnp.minimum(i + 1, n_tiles - 1)
    return pl.pallas_call(
        functools.partial(_outproj_conv_kernel, tc=tm, tiles_per_seq=tiles_per_seq),
        grid=(n_tiles,),
        in_specs=[
            pl.BlockSpec((tm, ATTN_W), row),
            pl.BlockSpec((tm, D_MODEL), row),
            _mod_spec(layer, tm, seq, n_batch),
            _layer_block((MIX_W, D_MODEL), layer, 2),
            pl.BlockSpec((tm, CONV_CH), lambda i: (0, 0), pipeline_mode=pl.Buffered(1)),
            pl.BlockSpec((CONV_HALO, CONV_CH), lambda i: (hb, 0)),
            pl.BlockSpec((tm, CONV_CH), lambda i: (nxt(i), 0)),
            pl.BlockSpec((CONV_HALO, CONV_CH), lambda i: (nxt(i) * hb - 1, 0)),
            pl.BlockSpec((CONV_HALO, CONV_CH), lambda i: ((nxt(i) + 1) * hb, 0)),
            _layer_block((CONV_K, CONV_CH), layer, 4),
            _layer_block((1, CONV_CH), layer, 4),
            _layer_block((1, CONV_CH), layer, 4),
            _layer_block((1, CONV_CH), layer, 4),
        ],
        out_specs=pl.BlockSpec((tm, D_MODEL), row),
        out_shape=jax.ShapeDtypeStruct((out_rows, D_MODEL), F32),
        scratch_shapes=[pltpu.VMEM((2, tm, CONV_CH), BF16)] + _conv_scratch(tm),
        input_output_aliases={1: 0} if res.shape[0] == out_rows else {},
        compiler_params=_params(1),
        name="outproj_conv",
    )(att, res, mod, w, u, u, u, u, u, cw, cb, lg, lb)


def _mlp_kernel(x_ref, mod_ref, g_ref, gf_ref, w1_ref, w2_ref, o_ref, h_s, *, final):
    j = pl.program_id(1)

    @pl.when(j == 0)
    def _prologue():
        _rms_modulate_to(h_s, x_ref, g_ref[...],
                         mod_ref[:, 3 * D_MODEL:4 * D_MODEL], mod_ref[:, 4 * D_MODEL:5 * D_MODEL])
        o_ref[...] = jnp.zeros_like(o_ref)

    a = jnp.dot(h_s[...], w1_ref[...], preferred_element_type=F32)
    a = jnp.maximum(a, 0.0)
    o_ref[...] += jnp.dot((a * a).astype(BF16), w2_ref[...], preferred_element_type=F32)

    @pl.when(j == pl.num_programs(1) - 1)
    def _epilogue():
        g2 = mod_ref[:, 5 * D_MODEL:6 * D_MODEL]
        if not final:
            o_ref[...] = x_ref[...] + g2 * o_ref[...]
        else:
            for r0 in range(0, o_ref.shape[0], NORM_ROWS):
                rows = slice(r0, r0 + NORM_ROWS)
                y = x_ref[rows, :] + g2 * o_ref[rows, :]
                rs = lax.rsqrt(jnp.mean(y * y, axis=-1, keepdims=True) + EPS)
                o_ref[rows, :] = y * rs * gf_ref[...]


def _mlp(xa, mod, g, gf, w1, w2, layer, n_lat_rows, seq, final):
    rows = xa.shape[0]
    tm, fc = TM_MLP, FC_MLP
    n_batch = n_lat_rows // seq
    row = lambda i, j: (i, 0)
    return pl.pallas_call(
        functools.partial(_mlp_kernel, final=final),
        grid=(rows // tm, D_FF // fc),
        in_specs=[
            pl.BlockSpec((tm, D_MODEL), row),
            _mod_spec(layer, tm, seq, n_batch),
            _layer_block((1, D_MODEL), layer, 4),
            pl.BlockSpec((1, D_MODEL), lambda i, j: (0, 0)),
            pl.BlockSpec((None, D_MODEL, fc), lambda i, j: (layer, 0, j)),
            pl.BlockSpec((None, fc, D_MODEL), lambda i, j: (layer, j, 0)),
        ],
        out_specs=pl.BlockSpec((tm, D_MODEL), row),
        out_shape=jax.ShapeDtypeStruct((rows, D_MODEL), F32),
        scratch_shapes=[pltpu.VMEM((tm, D_MODEL), BF16)],
        input_output_aliases={0: 0},
        compiler_params=_params(2),
        name="mlp",
    )(xa, mod, g, gf, w1, w2)


def _rope_tables(seq, pad_rows):
    t = jnp.arange(seq, dtype=jnp.int32)
    row = (t // GRID_W).astype(F32)
    col = (t % GRID_W).astype(F32)
    n_freq = HEAD_DIM // 4
    inv = ROPE_BASE ** (-jnp.arange(n_freq, dtype=F32) / n_freq)
    theta = jnp.concatenate([row[:, None] * inv, col[:, None] * inv], axis=-1)
    theta = jnp.concatenate([theta, theta], axis=-1)
    sign = jnp.concatenate([-jnp.ones((HEAD_DIM // 2,), F32), jnp.ones((HEAD_DIM // 2,), F32)])
    cos = jnp.concatenate([jnp.cos(theta), jnp.ones((pad_rows, HEAD_DIM), F32)], axis=0)
    sin = jnp.concatenate([jnp.sin(theta) * sign, jnp.zeros((pad_rows, HEAD_DIM), F32)], axis=0)
    return cos, sin


def kernel(x, c, ctx, c_ctx, w_ada, b_ada, g_mix, g_mlp, w_in, attn_sink, conv_w, conv_b,
           conv_ln_g, conv_ln_b, w_out, w_mlp1, w_mlp2, g_final):
    n_batch, seq, d = x.shape
    ctx_len = ctx.shape[1]
    n_lat_rows = n_batch * seq
    assert d == D_MODEL and n_batch + 1 <= MOD_ROWS
    assert seq % TM_MLP == 0 and (n_batch * ctx_len) % TM_MLP == 0 and TQ % ctx_len == 0

    n_ctx_rows = n_batch * ctx_len
    rows = n_lat_rows + n_ctx_rows
    x2 = x.reshape(n_lat_rows, d)
    ctx2 = ctx.reshape(n_ctx_rows, d)
    cc = jnp.concatenate([c, c_ctx[None], jnp.zeros((MOD_ROWS - n_batch - 1, d), F32)], axis=0)
    mod = _ada_table(cc, w_ada, b_ada).reshape(DEPTH, MOD_ROWS, 1, 6 * d)
    cos_t, sin_t = _rope_tables(seq, TM_IN)

    w_in_b = w_in.astype(BF16)
    w_out_b = w_out.astype(BF16)
    w1_b = w_mlp1.astype(BF16)
    w2_b = w_mlp2.astype(BF16)
    g_mix3 = g_mix.reshape(DEPTH, 1, d)
    g_mlp3 = g_mlp.reshape(DEPTH, 1, d)
    gf = g_final.reshape(1, d)
    conv_b3 = conv_b.reshape(DEPTH, 1, CONV_CH)
    conv_lg3 = conv_ln_g.reshape(DEPTH, 1, CONV_CH)
    conv_lb3 = conv_ln_b.reshape(DEPTH, 1, CONV_CH)

    srcs = (x2, ctx2)
    for l in range(DEPTH):
        last = l == DEPTH - 1
        q, k, v, u, *joined = _inproj(srcs, mod, g_mix3, cos_t, sin_t, w_in_b, l, rows, n_lat_rows, seq)
        xa = joined[0] if joined else srcs[0]
        att = _attention(attn_sink, q, k, v, l, n_lat_rows, seq, ctx_len, with_ctx=not last)
        conv_args = (conv_w, conv_b3, conv_lg3, conv_lb3, l)
        xa = _outproj_conv(att, u, xa, mod, w_out_b, *conv_args,
                           n_lat_rows if last else rows, n_lat_rows, seq)
        if not last:
            cv_ctx = _conv_call(u, *conv_args, ctx_len, n_lat_rows, n_ctx_rows, 1)
            xa = _outproj(att, cv_ctx, xa, n_lat_rows, n_ctx_rows, mod, w_out_b, l, n_lat_rows, seq)
        xa = _mlp(xa, mod, g_mlp3, gf, w1_b, w2_b, l, n_lat_rows, seq, last)
        srcs = (xa,)
    return xa.reshape(n_batch, seq, d)
```
